```python
import math
import jax, jax.numpy as jnp
from jax import lax
import numpy as np

D_MODEL = 1024
BATCH = 16
SEQ = 2048
DEPTH = 2

GRID_W = 64
CTX_LEN = 256
HA = 4
DA = 64
HB = 8
NOPE_B = 64
ROPE_B = 32
V_B = 64
Q_RANK = 256
KV_RANK = 128
HC = 8
KVC = 2
G_C = HC // KVC
DC = 64
WINDOW = 128
BRANCH_W = 512
N_BRANCH = 3
Q_BLOCK = 128
D_FF = 2816
CONV_W = 3
ROPE_BASE = 10000.0
EPS = 1e-6
NEG_INF = -1e30
N_MOD = 6
MLA_SCALE = (NOPE_B + ROPE_B) ** -0.5

IN_WIDTHS = (HA * 2 * DA, HA * 2 * DA, HA * 2 * DA, Q_RANK, KV_RANK, ROPE_B, HC * DC, KVC * DC, KVC * DC)
IN_SPLITS = tuple(int(v) for v in np.cumsum(IN_WIDTHS)[:-1])
N_IN = int(sum(IN_WIDTHS))

kernel_name = "hybrid_prefix_diffusion_block"


def rms_norm(x, g):
    xf = x.astype(jnp.float32)
    y = xf * lax.rsqrt(jnp.mean(xf * xf, axis=-1, keepdims=True) + EPS)
    return (y * g.astype(jnp.float32)).astype(x.dtype)


def modulate(x, shift, scale):
    return x * (1 + scale) + shift


def axial_rope_table(n_rows, rot_dim):
    n_freq = rot_dim // 4
    inv = ROPE_BASE ** (-jnp.arange(n_freq, dtype=jnp.float32) / n_freq)
    rows = jnp.repeat(jnp.arange(n_rows, dtype=jnp.float32), GRID_W)
    cols = jnp.tile(jnp.arange(GRID_W, dtype=jnp.float32), n_rows)
    ang = jnp.concatenate([rows[:, None] * inv, cols[:, None] * inv], axis=-1)
    return jnp.cos(ang), jnp.sin(ang)


def apply_rope(x, cos, sin):
    shape = (1, x.shape[1]) + (1,) * (x.ndim - 3) + (cos.shape[-1],)
    cs = cos.reshape(shape).astype(x.dtype)
    sn = sin.reshape(shape).astype(x.dtype)
    x1, x2 = jnp.split(x, 2, axis=-1)
    return jnp.concatenate([x1 * cs - x2 * sn, x2 * cs + x1 * sn], axis=-1)


def sweep_query_blocks(fn, *qs):
    n = qs[0].shape[1]
    nb = n // Q_BLOCK
    blocks = tuple(jnp.moveaxis(q.reshape(q.shape[0], nb, Q_BLOCK, *q.shape[2:]), 1, 0) for q in qs)
    out = lax.map(lambda args: fn(*args), blocks)
    out = jnp.moveaxis(out, 0, 1)
    return out.reshape(out.shape[0], n, *out.shape[3:])


def softmax_with_sink(s, sink):
    s_all = jnp.concatenate([s, jnp.broadcast_to(sink, s.shape[:-1] + (1,))], axis=-1)
    return jax.nn.softmax(s_all, axis=-1)[..., :-1]


def diff_core(q, k, v, lam):
    s = jnp.einsum('bqhid,bkhid->bhiqk', q, k, preferred_element_type=jnp.float32) * (DA ** -0.5)
    p = jax.nn.softmax(s, axis=-1)
    w = p[:, :, 0] - lam * p[:, :, 1]
    return jnp.einsum('bhqk,bkhd->bqhd', w.astype(v.dtype), v)


def diff_attention(a_lat, a_ctx, cos, sin, lam_p, g_diff, lam_init, ctx_queries):
    def heads(q, k, v):
        b, s, _ = q.shape
        return (q.reshape(b, s, HA, 2, DA), k.reshape(b, s, HA, 2, DA), v.reshape(b, s, HA, 2 * DA))
    q, k, v = heads(*a_lat)
    cq, ck, cv = heads(*a_ctx)
    q = apply_rope(q, cos, sin)
    k = apply_rope(k, cos, sin)
    lp = lam_p.astype(jnp.float32)
    lam = jnp.exp(jnp.sum(lp[0] * lp[1])) - jnp.exp(jnp.sum(lp[2] * lp[3])) + lam_init
    kk = jnp.concatenate([ck, k], axis=1)
    vv = jnp.concatenate([cv, v], axis=1)

    def finish(o):
        b, s = o.shape[:2]
        return (rms_norm(o, g_diff) * (1.0 - lam_init)).reshape(b, s, HA * 2 * DA)
    y = finish(sweep_query_blocks(lambda qb: diff_core(qb, kk, vv, lam), q))
    yc = finish(diff_core(cq, ck, cv, lam)) if ctx_queries else None
    return y, yc


def mla_core(qn, qr, kn, kr, v):
    s = (jnp.einsum('bqhd,bkhd->bhqk', qn, kn, preferred_element_type=jnp.float32)
         + jnp.einsum('bqhd,bkd->bhqk', qr, kr, preferred_element_type=jnp.float32)) * MLA_SCALE
    p = jax.nn.softmax(s, axis=-1)
    return jnp.einsum('bhqk,bkhd->bqhd', p.astype(v.dtype), v)


def mla(b_lat, b_ctx, cos, sin, g_cq, w_uq, g_ckv, w_ukv, ctx_queries):
    def qry(c_q):
        b, s, _ = c_q.shape
        qh = (rms_norm(c_q, g_cq) @ w_uq).reshape(b, s, HB, NOPE_B + ROPE_B)
        return qh[..., :NOPE_B], qh[..., NOPE_B:]

    def kv(c_kv):
        b, s, _ = c_kv.shape
        kvh = (rms_norm(c_kv, g_ckv) @ w_ukv).reshape(b, s, HB, NOPE_B + V_B)
        return kvh[..., :NOPE_B], kvh[..., NOPE_B:]
    cq_l, ckv_l, kr_l = b_lat
    cq_c, ckv_c, kr_c = b_ctx
    kn_l, v_l = kv(ckv_l)
    kn_c, v_c = kv(ckv_c)
    kr_l = apply_rope(kr_l, cos, sin)
    kn = jnp.concatenate([kn_c, kn_l], axis=1)
    kr = jnp.concatenate([kr_c, kr_l], axis=1)
    vv = jnp.concatenate([v_c, v_l], axis=1)
    qn, qr = qry(cq_l)
    qr = apply_rope(qr, cos, sin)
    y = sweep_query_blocks(lambda a, r: mla_core(a, r, kn, kr, vv), qn, qr)
    y = y.reshape(y.shape[0], y.shape[1], HB * V_B)
    yc = None
    if ctx_queries:
        qn_c, qr_c = qry(cq_c)
        yc = mla_core(qn_c, qr_c, kn_c, kr_c, v_c)
        yc = yc.reshape(yc.shape[0], yc.shape[1], HB * V_B)
    return y, yc


def window_gqa(c_lat, c_ctx, cos, sin, sink, ctx_queries):
    def heads(q, k, v):
        b, s, _ = q.shape
        return (q.reshape(b, s, KVC, G_C, DC), k.reshape(b, s, KVC, DC), v.reshape(b, s, KVC, DC))
    q, k, v = heads(*c_lat)
    cq, ck, cv = heads(*c_ctx)
    q = apply_rope(q, cos, sin)
    k = apply_rope(k, cos, sin)
    b, s = q.shape[:2]
    nb = s // WINDOW
    n_ctx = ck.shape[1]
    scale = DC ** -0.5
    sink_b = sink.astype(jnp.float32).reshape(KVC, G_C, 1, 1)

    def band(t):
        tb = t.reshape(b, nb, WINDOW, KVC, DC)
        zeros = jnp.zeros_like(tb[:, :1])
        prev = jnp.concatenate([zeros, tb[:, :-1]], axis=1)
        nxt = jnp.concatenate([tb[:, 1:], zeros], axis=1)
        return jnp.concatenate([prev, tb, nxt], axis=2)
    qb = jnp.moveaxis(q.reshape(b, nb, WINDOW, KVC, G_C, DC), 1, 0)
    kb = jnp.moveaxis(band(k), 1, 0)
    vb = jnp.moveaxis(band(v), 1, 0)

    def block(args):
        n, qn, kn, vn = args
        qpos = n * WINDOW + jnp.arange(WINDOW)
        kpos = (n - 1) * WINDOW + jnp.arange(3 * WINDOW)
        ok = (jnp.abs(qpos[:, None] - kpos[None, :]) <= WINDOW) & (kpos[None, :] >= 0) & (kpos[None, :] < s)
        s_c = jnp.einsum('bqgrd,bkgd->bgrqk', qn, ck, preferred_element_type=jnp.float32) * scale
        s_b = jnp.einsum('bqgrd,bkgd->bgrqk', qn, kn, preferred_element_type=jnp.float32) * scale
        s_b = jnp.where(ok, s_b, NEG_INF)
        p = softmax_with_sink(jnp.concatenate([s_c, s_b], axis=-1), sink_b).astype(vn.dtype)
        return (jnp.einsum('bgrqk,bkgd->bqgrd', p[..., :n_ctx], cv)
                + jnp.einsum('bgrqk,bkgd->bqgrd', p[..., n_ctx:], vn))
    out = lax.map(block, (jnp.arange(nb), qb, kb, vb))
    y = jnp.moveaxis(out, 0, 1).reshape(b, s, HC * DC)
    yc = None
    if ctx_queries:
        s_cc = jnp.einsum('bqgrd,bkgd->bgrqk', cq, ck, preferred_element_type=jnp.float32) * scale
        p_cc = softmax_with_sink(s_cc, sink_b).astype(cv.dtype)
        yc = jnp.einsum('bgrqk,bkgd->bqgrd', p_cc, cv).reshape(b, n_ctx, HC * DC)
    return y, yc


def merge_branches(u, ys, w_branch, w_gate, b_gate, w_o):
    gates = jax.nn.sigmoid(u @ w_gate + b_gate).reshape(*u.shape[:-1], N_BRANCH, D_MODEL)
    proj = jnp.einsum('bsnc,ncd->bsnd', jnp.stack(ys, axis=-2), w_branch)
    return jnp.sum(gates * proj, axis=-2) @ w_o


def dwconv(t, w, bias):
    pad = (CONV_W - 1) // 2
    y = lax.conv_general_dilated(t, w[:, None, :].astype(t.dtype), window_strides=(1,), padding=[(pad, pad)],
                                 dimension_numbers=('NWC', 'WIO', 'NWC'), feature_group_count=t.shape[-1])
    return y + bias


def conv_ffn(t, w_up, conv_w, conv_b, w_down):
    a = dwconv(t @ w_up, conv_w, conv_b)
    gate, val = jnp.split(a, 2, axis=-1)
    return (jax.nn.silu(gate) * val) @ w_down


def setup_inputs(seed: int = 0) -> dict:
    key = jax.random.key(seed)
    ks = jax.random.split(key, 32)
    f32 = jnp.float32
    nrm = lambda k, shape: jax.random.normal(k, shape, f32)
    L, D = DEPTH, D_MODEL
    return {
        "x": nrm(ks[0], (BATCH, SEQ, D)),
        "c": nrm(ks[1], (BATCH, D)),
        "ctx": nrm(ks[2], (BATCH, CTX_LEN, D)),
        "c_ctx": nrm(ks[3], (D,)),
        "w_ada": nrm(ks[4], (L, D, N_MOD * D)) * (0.5 * D ** -0.5),
        "b_ada": nrm(ks[5], (L, N_MOD * D)) * 0.02,
        "g_norm1": 1.0 + 0.02 * nrm(ks[6], (L, D)),
        "w_in": nrm(ks[7], (L, D, N_IN)) * D ** -0.5,
        "lam": nrm(ks[8], (L, 4, DA)) * 0.1,
        "g_diff": 1.0 + 0.02 * nrm(ks[9], (L, 2 * DA)),
        "g_cq": 1.0 + 0.02 * nrm(ks[10], (L, Q_RANK)),
        "w_uq": nrm(ks[11], (L, Q_RANK, HB * (NOPE_B + ROPE_B))) * Q_RANK ** -0.5,
        "g_ckv": 1.0 + 0.02 * nrm(ks[12], (L, KV_RANK)),
        "w_ukv": nrm(ks[13], (L, KV_RANK, HB * (NOPE_B + V_B))) * KV_RANK ** -0.5,
        "sink": nrm(ks[14], (L, HC)) * 0.5,
        "w_branch": nrm(ks[15], (L, N_BRANCH, BRANCH_W, D)) * BRANCH_W ** -0.5,
        "w_gate": nrm(ks[16], (L, D, N_BRANCH * D)) * D ** -0.5,
        "b_gate": nrm(ks[17], (L, N_BRANCH * D)) * 0.02,
        "w_o": nrm(ks[18], (L, D, D)) * D ** -0.5,
        "g_norm2": 1.0 + 0.02 * nrm(ks[19], (L, D)),
        "w_up": nrm(ks[20], (L, D, 2 * D_FF)) * D ** -0.5,
        "conv_w": nrm(ks[21], (L, CONV_W, 2 * D_FF)) * CONV_W ** -0.5,
        "conv_b": nrm(ks[22], (L, 2 * D_FF)) * 0.02,
        "w_down": nrm(ks[23], (L, D_FF, D)) * D_FF ** -0.5,
        "g_final": 1.0 + 0.02 * nrm(ks[24], (D,)),
    }


def reference(x, c, ctx, c_ctx, w_ada, b_ada, g_norm1, w_in, lam, g_diff, g_cq, w_uq, g_ckv, w_ukv, sink,
              w_branch, w_gate, b_gate, w_o, g_norm2, w_up, conv_w, conv_b, w_down, g_final):
    b = x.shape[0]
    n_rows = x.shape[1] // GRID_W
    cos64, sin64 = axial_rope_table(n_rows, DA)
    cos_b, sin_b = axial_rope_table(n_rows, ROPE_B)
    h, hc = x, ctx
    for l in range(DEPTH):
        ctx_out = l < DEPTH - 1
        lam_init = 0.8 - 0.6 * math.exp(-0.3 * l)
        mod = (jax.nn.silu(c) @ w_ada[l] + b_ada[l]).reshape(b, 1, N_MOD, D_MODEL)
        modc = (jax.nn.silu(c_ctx) @ w_ada[l] + b_ada[l]).reshape(N_MOD, D_MODEL)
        u = modulate(rms_norm(h, g_norm1[l]), mod[:, :, 0], mod[:, :, 1])
        uc = modulate(rms_norm(hc, g_norm1[l]), modc[0], modc[1])
        pl = jnp.split(u @ w_in[l], IN_SPLITS, axis=-1)
        pc = jnp.split(uc @ w_in[l], IN_SPLITS, axis=-1)
        ya, ya_c = diff_attention(pl[0:3], pc[0:3], cos64, sin64, lam[l], g_diff[l], lam_init, ctx_out)
        yb, yb_c = mla(pl[3:6], pc[3:6], cos_b, sin_b, g_cq[l], w_uq[l], g_ckv[l], w_ukv[l], ctx_out)
        yw, yw_c = window_gqa(pl[6:9], pc[6:9], cos64, sin64, sink[l], ctx_out)
        h = h + mod[:, :, 2] * merge_branches(u, (ya, yb, yw), w_branch[l], w_gate[l], b_gate[l], w_o[l])
        u2 = modulate(rms_norm(h, g_norm2[l]), mod[:, :, 3], mod[:, :, 4])
        h = h + mod[:, :, 5] * conv_ffn(u2, w_up[l], conv_w[l], conv_b[l], w_down[l])
        if ctx_out:
            hc = hc + modc[2] * merge_branches(uc, (ya_c, yb_c, yw_c), w_branch[l], w_gate[l], b_gate[l], w_o[l])
            uc2 = modulate(rms_norm(hc, g_norm2[l]), modc[3], modc[4])
            hc = hc + modc[5] * conv_ffn(uc2, w_up[l], conv_w[l], conv_b[l], w_down[l])
    return rms_norm(h, g_final)
```

```python
import functools
import math

import numpy as np
import jax
import jax.numpy as jnp
from jax import lax
from jax.experimental import pallas as pl
from jax.experimental.pallas import tpu as pltpu

F32 = jnp.float32
MXU_DTYPE = jnp.bfloat16

GRID_W = 64
HA, DA = 4, 64
HB, NOPE_B, ROPE_B, V_B = 8, 64, 32, 64
Q_RANK, KV_RANK = 256, 128
HC, KVC, DC = 8, 2, 64
G_C = HC // KVC
WINDOW = 128
BRANCH_W = 512
N_BRANCH = 3
CONV_W = 3
ROPE_BASE = 10000.0
EPS = 1e-6
NEG_INF = -1e30
N_MOD = 6
MLA_SCALE = (NOPE_B + ROPE_B) ** -0.5
IN_WIDTHS = (HA * 2 * DA, HA * 2 * DA, HA * 2 * DA, Q_RANK, KV_RANK, ROPE_B, HC * DC, KVC * DC, KVC * DC)
IN_OFFS = tuple(int(v) for v in np.cumsum((0,) + IN_WIDTHS))

LANES = 128
SUBLANES = 8
HALF = LANES // 2
VMEM_LIMIT = 56 * 1024 * 1024

SEG_QA, SEG_KA, SEG_VA, SEG_QC = 0, 512, 1024, 1536
SEG_KC, SEG_VC, SEG_CQ, SEG_CKV, SEG_KR = 2048, 2176, 2304, 2560, 2688
NP_IN = 2816


def _slab64_src(base, unit_of_group):
    src = np.zeros(LANES, np.int64)
    for grp in range(4):
        unit, half = grp % 2, grp // 2
        for r in range(32):
            src[grp * 32 + r] = base + unit_of_group[unit] + half * 32 + r
    return src


def _in_proj_src():
    src = -np.ones(NP_IN, np.int64)
    qa0, ka0, va0, cq0, ckv0, kr0, qc0, kc0, vc0 = IN_OFFS[:9]
    for h in range(HA):
        src[SEG_QA + h * LANES:SEG_QA + (h + 1) * LANES] = _slab64_src(qa0 + h * 2 * DA, (0, DA))
        src[SEG_KA + h * LANES:SEG_KA + (h + 1) * LANES] = _slab64_src(ka0 + h * 2 * DA, (0, DA))
    src[SEG_VA:SEG_VA + 512] = va0 + np.arange(512)
    for g in range(G_C):
        src[SEG_QC + g * LANES:SEG_QC + (g + 1) * LANES] = _slab64_src(qc0, (g * DC, (G_C + g) * DC))
    src[SEG_KC:SEG_KC + LANES] = _slab64_src(kc0, (0, DC))
    src[SEG_VC:SEG_VC + LANES] = vc0 + np.arange(LANES)
    src[SEG_CQ:SEG_CQ + Q_RANK] = cq0 + np.arange(Q_RANK)
    src[SEG_CKV:SEG_CKV + KV_RANK] = ckv0 + np.arange(KV_RANK)
    src[SEG_KR:SEG_KR + 16] = kr0 + np.arange(16)
    src[SEG_KR + HALF:SEG_KR + HALF + 16] = kr0 + 16 + np.arange(16)
    return src


def _mla_slab_src(nope0, rope0):
    src = -np.ones(LANES, np.int64)
    if rope0 is not None:
        src[0:16] = rope0 + np.arange(16)
        src[HALF:HALF + 16] = rope0 + 16 + np.arange(16)
    src[16:64] = nope0 + np.arange(48)
    src[80:96] = nope0 + 48 + np.arange(16)
    return src


def _wuq_src():
    src = -np.ones(HB * LANES, np.int64)
    for h in range(HB):
        b0 = h * (NOPE_B + ROPE_B)
        src[h * LANES:(h + 1) * LANES] = _mla_slab_src(b0, b0 + NOPE_B)
    return src


def _wkv_src():
    src = -np.ones(2 * HB * LANES, np.int64)
    for h in range(HB):
        b0 = h * (NOPE_B + V_B)
        src[h * LANES:(h + 1) * LANES] = _mla_slab_src(b0, None)
        v_lo = HB * LANES + h * LANES + (h % 2) * HALF
        src[v_lo:v_lo + V_B] = b0 + NOPE_B + np.arange(V_B)
    return src


def _wbranch_c_src():
    src = np.zeros(BRANCH_W, np.int64)
    for g in range(G_C):
        for kv in range(KVC):
            src[g * LANES + kv * DC:g * LANES + (kv + 1) * DC] = (kv * G_C + g) * DC + np.arange(DC)
    return src


def _take_cols(w, src):
    cols = jnp.take(w, jnp.asarray(np.maximum(src, 0), jnp.int32), axis=-1)
    return jnp.where(jnp.asarray(src >= 0)[None, :], cols, 0.0)


def _rope_tables(n_tok):
    def table(rot_dim):
        n_freq = rot_dim // 4
        inv = ROPE_BASE ** (-jnp.arange(n_freq, dtype=F32) / n_freq)
        n_rows = n_tok // GRID_W
        rows = jnp.repeat(jnp.arange(n_rows, dtype=F32), GRID_W)
        cols = jnp.tile(jnp.arange(GRID_W, dtype=F32), n_rows)
        ang = jnp.concatenate([rows[:, None] * inv, cols[:, None] * inv], axis=-1)
        return jnp.cos(ang), jnp.sin(ang)

    cos64, sin64 = table(DA)
    cosb, sinb = table(ROPE_B)
    cos64_t = jnp.concatenate([cos64] * 4, axis=-1)
    sin64_t = jnp.concatenate([-sin64, -sin64, sin64, sin64], axis=-1)
    ones = jnp.ones((n_tok, HALF - 16), F32)
    zeros = jnp.zeros((n_tok, HALF - 16), F32)
    cosb_t = jnp.concatenate([cosb, ones, cosb, ones], axis=-1)
    sinb_t = jnp.concatenate([-sinb, zeros, sinb, zeros], axis=-1)
    return cos64_t, sin64_t, cosb_t, sinb_t


def _mm(a, b):
    return jnp.dot(a.astype(MXU_DTYPE), b.astype(MXU_DTYPE), preferred_element_type=F32)


def _mm_t(a, b):
    return lax.dot_general(a.astype(MXU_DTYPE), b.astype(MXU_DTYPE), (((1,), (1,)), ((), ())),
                           preferred_element_type=F32)


def _rms(x, g):
    return x * lax.rsqrt(jnp.mean(x * x, axis=-1, keepdims=True) + EPS) * g


def _norm_mod(x, g, shift, scale):
    return _rms(x, g) * (1.0 + scale) + shift


def _sigmoid(x):
    return 1.0 / (1.0 + jnp.exp(-x))


def _rope_slab(x, cos_t, sin_t):
    return x * cos_t + pltpu.roll(x, HALF, axis=1) * sin_t


def _resident(shape):
    nd = len(shape)
    return pl.BlockSpec(shape, lambda *_: (0,) * nd, pipeline_mode=pl.Buffered(1))


def _params(n_grid):
    return pltpu.CompilerParams(dimension_semantics=("arbitrary",) * n_grid, vmem_limit_bytes=VMEM_LIMIT)


def _mod_kernel(c_ref, w_ref, b_ref, o_ref):
    c = c_ref[...]
    s = c * _sigmoid(c)
    o_ref[...] = _mm(s, w_ref[...]) + b_ref[...]


def _modulation(c_all, w_ada, b_ada):
    n_layers, d, n_out = w_ada.shape
    rows = c_all.shape[0]
    tn = 1536
    return pl.pallas_call(
        _mod_kernel,
        out_shape=jax.ShapeDtypeStruct((n_layers, rows, n_out), F32),
        grid=(n_layers, n_out // tn),
        in_specs=[
            pl.BlockSpec((rows, d), lambda l, j: (0, 0)),
            pl.BlockSpec((None, d, tn), lambda l, j: (l, 0, j)),
            pl.BlockSpec((None, 1, tn), lambda l, j: (l, 0, j)),
        ],
        out_specs=pl.BlockSpec((None, rows, tn), lambda l, j: (l, 0, j)),
        compiler_params=_params(2),
        name="modulation",
    )(c_all, w_ada, b_ada.reshape(n_layers, 1, n_out))


def _inproj_kernel(*refs, rope, need_q):
    it = iter(refs)
    h_ref, mod_ref, g1_ref, w_ref, gcq_ref, wuq_ref, gckv_ref, wkv_ref = (next(it) for _ in range(8))
    if rope:
        cos64_ref, sin64_ref, cosb_ref, sinb_ref = (next(it) for _ in range(4))
    if need_q:
        qa_ref, qb_ref, qc_ref = (next(it) for _ in range(3))
    ka_ref, va_ref, kb_ref, vb_ref, kc_ref, vc_ref = (next(it) for _ in range(6))

    u = _norm_mod(h_ref[...], g1_ref[...], mod_ref[0:1, :], mod_ref[1:2, :]).astype(MXU_DTYPE)

    def proj(lo, width):
        return jnp.dot(u, w_ref[:, lo:lo + width], preferred_element_type=F32)

    def rot64(x):
        return _rope_slab(x, cos64_ref[...], sin64_ref[...]) if rope else x

    def rotb(x):
        return _rope_slab(x, cosb_ref[...], sinb_ref[...]) if rope else x

    def store_rot64(dst_ref, lo, n_slab, scale):
        p = proj(lo, n_slab * LANES)
        for s in range(n_slab):
            y = rot64(p[:, s * LANES:(s + 1) * LANES])
            if scale != 1.0:
                y = y * scale
            dst_ref[:, s * LANES:(s + 1) * LANES] = y.astype(dst_ref.dtype)

    if need_q:
        store_rot64(qa_ref, SEG_QA, HA, DA ** -0.5)
        store_rot64(qc_ref, SEG_QC, G_C, DC ** -0.5)
    store_rot64(ka_ref, SEG_KA, HA, 1.0)
    va_ref[...] = proj(SEG_VA, 512).astype(va_ref.dtype)
    kcvc = proj(SEG_KC, 2 * LANES)
    kc_ref[...] = rot64(kcvc[:, :LANES]).astype(kc_ref.dtype)
    vc_ref[...] = kcvc[:, LANES:].astype(vc_ref.dtype)

    low = proj(SEG_CQ, Q_RANK + KV_RANK + LANES)
    if need_q:
        cq = _rms(low[:, :Q_RANK], gcq_ref[...])
        qh = _mm(cq, wuq_ref[...])
        for h in range(HB):
            y = rotb(qh[:, h * LANES:(h + 1) * LANES]) * MLA_SCALE
            qb_ref[:, h * LANES:(h + 1) * LANES] = y.astype(qb_ref.dtype)
    ckv = _rms(low[:, Q_RANK:Q_RANK + KV_RANK], gckv_ref[...])
    kvh = _mm(ckv, wkv_ref[...])
    kr = rotb(low[:, Q_RANK + KV_RANK:])
    for h in range(HB):
        kb_ref[:, h * LANES:(h + 1) * LANES] = (kvh[:, h * LANES:(h + 1) * LANES] + kr).astype(kb_ref.dtype)
    vb_ref[...] = kvh[:, HB * LANES:].astype(vb_ref.dtype)


def _inproj(h, mod_l, mod_row0, wts, tables, tm, need_q):
    nb, ns, d = h.shape
    rope = tables is not None
    tok = lambda w: pl.BlockSpec((None, tm, w), lambda b, i: (b, i, 0))
    in_specs = [
        tok(d),
        pl.BlockSpec((None, N_MOD, d), (lambda b, i: (b + 1, 0, 0)) if mod_row0 else (lambda b, i: (0, 0, 0))),
        _resident((1, d)), _resident((d, NP_IN)),
        _resident((1, Q_RANK)), _resident((Q_RANK, HB * LANES)),
        _resident((1, KV_RANK)), _resident((KV_RANK, 2 * HB * LANES)),
    ]
    args = [h, mod_l, wts["g1"], wts["w_in"], wts["g_cq"], wts["w_uq"], wts["g_ckv"], wts["w_kv"]]
    if rope:
        in_specs += [pl.BlockSpec((tm, LANES), lambda b, i: (i, 0))] * 4
        args += list(tables)
    names, widths = [], []
    if need_q:
        names += ["qa", "qb", "qc"]
        widths += [512, HB * LANES, 512]
    names += ["ka", "va", "kb", "vb", "kc", "vc"]
    widths += [512, 512, HB * LANES, HB * LANES, LANES, LANES]
    outs = pl.pallas_call(
        functools.partial(_inproj_kernel, rope=rope, need_q=need_q),
        out_shape=[jax.ShapeDtypeStruct((nb, ns, w), MXU_DTYPE) for w in widths],
        grid=(nb, ns // tm),
        in_specs=in_specs,
        out_specs=[tok(w) for w in widths],
        compiler_params=_params(2),
        name="inproj",
    )(*args)
    return dict(zip(names, outs))


def _softmax_parts(scores, extra=None):
    m = functools.reduce(jnp.maximum, [jnp.max(s, axis=-1, keepdims=True) for s in scores])
    if extra is not None:
        m = jnp.maximum(m, extra)
    ps = [jnp.exp(s - m) for s in scores]
    denom = functools.reduce(lambda a, b: a + b, [jnp.sum(p, axis=-1, keepdims=True) for p in ps])
    if extra is not None:
        denom = denom + jnp.exp(extra - m)
    return ps, denom


def _unit_mask(unit):
    lane = lax.broadcasted_iota(jnp.int32, (1, LANES), 1)
    return jnp.where((lane // 32) % 2 == unit, 1.0, 0.0).astype(MXU_DTYPE)


def _attn_a_kernel(*refs, lam_init, has_lat):
    if has_lat:
        lam_ref, gd_ref, q_ref, kc_ref, vc_ref, kl_ref, vl_ref, o_ref = refs
    else:
        lam_ref, gd_ref, q_ref, kc_ref, vc_ref, o_ref = refs
    tq = q_ref.shape[0]
    lp = lam_ref[...]
    lam = (jnp.exp(jnp.sum(lp[0:1] * lp[1:2], axis=-1, keepdims=True))
           - jnp.exp(jnp.sum(lp[2:3] * lp[3:4], axis=-1, keepdims=True)) + lam_init)
    m0, m1 = _unit_mask(0), _unit_mask(1)
    for h in range(HA):
        sl = slice(h * LANES, (h + 1) * LANES)
        q = q_ref[:, sl]
        qq = jnp.concatenate([q * m0, q * m1], axis=0)
        k_blocks = [kc_ref[:, sl]] + ([kl_ref[:, sl]] if has_lat else [])
        v_blocks = [vc_ref[:, sl]] + ([vl_ref[:, sl]] if has_lat else [])
        ps, denom = _softmax_parts([_mm_t(qq, k) for k in k_blocks])
        r = 1.0 / denom
        a0, a1 = r[:tq], lam * r[tq:]
        o = None
        for p, v in zip(ps, v_blocks):
            w = p[:tq] * a0 - p[tq:] * a1
            pv = _mm(w, v)
            o = pv if o is None else o + pv
        o = _rms(o, gd_ref[...]) * (1.0 - lam_init)
        o_ref[:, sl] = o.astype(o_ref.dtype)


def _attn_b_kernel(*refs, has_lat):
    if has_lat:
        q_ref, kc_ref, vc_ref, kl_ref, vl_ref, o_ref = refs
    else:
        q_ref, kc_ref, vc_ref, o_ref = refs
    for pair in range(HB // 2):
        o_pair = None
        for h in (2 * pair, 2 * pair + 1):
            sl = slice(h * LANES, (h + 1) * LANES)
            q = q_ref[:, sl]
            k_blocks = [kc_ref[:, sl]] + ([kl_ref[:, sl]] if has_lat else [])
            v_blocks = [vc_ref[:, sl]] + ([vl_ref[:, sl]] if has_lat else [])
            ps, denom = _softmax_parts([_mm_t(q, k) for k in k_blocks])
            o = None
            for p, v in zip(ps, v_blocks):
                pv = _mm(p, v)
                o = pv if o is None else o + pv
            o = o * (1.0 / denom)
            o_pair = o if o_pair is None else o_pair + o
        o_ref[:, pair * LANES:(pair + 1) * LANES] = o_pair.astype(o_ref.dtype)


def _attn_c_kernel(*refs, has_lat, n_lat):
    if has_lat:
        sink_ref, q_ref, kc_ref, vc_ref, kl_ref, vl_ref, o_ref = refs
    else:
        sink_ref, q_ref, kc_ref, vc_ref, o_ref = refs
    tq = q_ref.shape[0]
    k_blocks, v_blocks, ok = [kc_ref[...]], [vc_ref[...]], None
    if has_lat:
        band = tq + 2 * WINDOW
        q0 = pl.program_id(1) * tq
        start = pl.multiple_of(jnp.minimum(jnp.maximum(q0 - WINDOW, 0), n_lat - band), WINDOW)
        k_blocks.append(kl_ref[pl.ds(start, band), :])
        v_blocks.append(vl_ref[pl.ds(start, band), :])
        qpos = q0 + lax.broadcasted_iota(jnp.int32, (tq, 1), 0)
        kpos = start + lax.broadcasted_iota(jnp.int32, (1, band), 1)
        ok = jnp.abs(qpos - kpos) <= WINDOW
    lane = lax.broadcasted_iota(jnp.int32, (1, LANES), 1)
    masks = (_unit_mask(0), _unit_mask(1))
    for g in range(G_C):
        sl = slice(g * LANES, (g + 1) * LANES)
        q = q_ref[:, sl]
        o_kv = []
        for kv in range(KVC):
            qm = q * masks[kv]
            scores = [_mm_t(qm, k_blocks[0])]
            if has_lat:
                scores.append(jnp.where(ok, _mm_t(qm, k_blocks[1]), NEG_INF))
            sink = jnp.full((1, 1), sink_ref[kv * G_C + g], F32)
            ps, denom = _softmax_parts(scores, extra=sink)
            o = None
            for p, v in zip(ps, v_blocks):
                pv = _mm(p, v)
                o = pv if o is None else o + pv
            o_kv.append(o * (1.0 / denom))
        o_ref[:, sl] = jnp.where(lane < DC, o_kv[0], o_kv[1]).astype(o_ref.dtype)


def _attention(kind, q, kv_ctx, kv_lat, tq, extra_args, extra_specs, **static):
    nb, nq, qw = q.shape
    has_lat = kv_lat is not None
    kw = kv_ctx[0].shape[-1]
    full = lambda n: pl.BlockSpec((None, n, kw), lambda b, i: (b, 0, 0))
    in_specs = list(extra_specs) + [pl.BlockSpec((None, tq, qw), lambda b, i: (b, i, 0))]
    in_specs += [full(kv_ctx[0].shape[1])] * 2
    args = list(extra_args) + [q, kv_ctx[0], kv_ctx[1]]
    if has_lat:
        in_specs += [full(kv_lat[0].shape[1])] * 2
        args += [kv_lat[0], kv_lat[1]]
    body = {"a": _attn_a_kernel, "b": _attn_b_kernel, "c": _attn_c_kernel}[kind]
    return pl.pallas_call(
        functools.partial(body, has_lat=has_lat, **static),
        out_shape=jax.ShapeDtypeStruct((nb, nq, BRANCH_W), MXU_DTYPE),
        grid=(nb, nq // tq),
        in_specs=in_specs,
        out_specs=pl.BlockSpec((None, tq, BRANCH_W), lambda b, i: (b, i, 0)),
        compiler_params=_params(2),
        name="attn_" + kind,
    )(*args)


def _mixers(qs, ctx_kv, lat_kv, wts, lam_init, tq):
    has_lat = lat_kv is not None
    pick = lambda d, k, v: (d[k], d[v])
    ya = _attention("a", qs["qa"], pick(ctx_kv, "ka", "va"), pick(lat_kv, "ka", "va") if has_lat else None, tq,
                    [wts["lam"], wts["g_diff"]], [_resident((4, DA)), _resident((1, 2 * DA))], lam_init=lam_init)
    yb = _attention("b", qs["qb"], pick(ctx_kv, "kb", "vb"), pick(lat_kv, "kb", "vb") if has_lat else None, tq,
                    [], [])
    n_lat = lat_kv["kc"].shape[1] if has_lat else 0
    yc = _attention("c", qs["qc"], pick(ctx_kv, "kc", "vc"), pick(lat_kv, "kc", "vc") if has_lat else None, tq,
                    [wts["sink"]], [pl.BlockSpec(memory_space=pltpu.SMEM)], n_lat=n_lat)
    return ya, yb, yc


def _merge_kernel(h_ref, mod_ref, g1_ref, ya_ref, yb_ref, yc_ref, wg_ref, bg_ref, wb_ref, wo_ref, o_ref):
    x = h_ref[...]
    d = x.shape[-1]
    u = _norm_mod(x, g1_ref[...], mod_ref[0:1, :], mod_ref[1:2, :]).astype(MXU_DTYPE)
    acc = None
    for n, y_ref in enumerate((ya_ref, yb_ref, yc_ref)):
        gate = _sigmoid(jnp.dot(u, wg_ref[:, n * d:(n + 1) * d], preferred_element_type=F32)
                        + bg_ref[:, n * d:(n + 1) * d])
        term = gate * jnp.dot(y_ref[...], wb_ref[n], preferred_element_type=F32)
        acc = term if acc is None else acc + term
    o_ref[...] = x + mod_ref[2:3, :] * _mm(acc, wo_ref[...])


def _merge(h, mod_l, mod_row0, wts, ys, tm):
    nb, ns, d = h.shape
    tok = lambda w: pl.BlockSpec((None, tm, w), lambda b, i: (b, i, 0))
    return pl.pallas_call(
        _merge_kernel,
        out_shape=jax.ShapeDtypeStruct(h.shape, F32),
        grid=(nb, ns // tm),
        in_specs=[
            tok(d),
            pl.BlockSpec((None, N_MOD, d), (lambda b, i: (b + 1, 0, 0)) if mod_row0 else (lambda b, i: (0, 0, 0))),
            _resident((1, d)), tok(BRANCH_W), tok(BRANCH_W), tok(BRANCH_W),
            _resident((d, N_BRANCH * d)), _resident((1, N_BRANCH * d)),
            _resident((N_BRANCH, BRANCH_W, d)), _resident((d, d)),
        ],
        out_specs=tok(d),
        compiler_params=_params(2),
        name="merge",
    )(h, mod_l, wts["g1"], *ys, wts["w_gate"], wts["b_gate"], wts["w_branch"], wts["w_o"])


FF_CHUNK = 512


def _ffn_kernel(*refs, final_norm):
    if final_norm:
        h_ref, hp_ref, hn_ref, mod_ref, g2_ref, wu_ref, cw_ref, cb_ref, wd_ref, gf_ref, o_ref = refs
    else:
        h_ref, hp_ref, hn_ref, mod_ref, g2_ref, wu_ref, cw_ref, cb_ref, wd_ref, o_ref = refs
    i, n_tiles = pl.program_id(1), pl.num_programs(1)
    x = h_ref[...]
    tm, d = x.shape
    d_ff = wd_ref.shape[0]
    shift, scale, g2 = mod_ref[3:4, :], mod_ref[4:5, :], g2_ref[...]
    u_prev = jnp.where(i > 0, _norm_mod(hp_ref[...], g2, shift, scale), 0.0)
    u_next = jnp.where(i < n_tiles - 1, _norm_mod(hn_ref[...], g2, shift, scale), 0.0)
    u = jnp.concatenate([u_prev, _norm_mod(x, g2, shift, scale), u_next], axis=0).astype(MXU_DTYPE)
    n_ext = tm + 2 * SUBLANES

    def conv(a, lo, width):
        w = cw_ref[:, lo:lo + width]
        prev = pltpu.roll(a, 1, axis=0)[SUBLANES:SUBLANES + tm]
        nxt = pltpu.roll(a, n_ext - 1, axis=0)[SUBLANES:SUBLANES + tm]
        return (w[0:1] * prev + w[1:2] * a[SUBLANES:SUBLANES + tm] + w[2:3] * nxt) + cb_ref[:, lo:lo + width]

    acc = None
    for lo in range(0, d_ff, FF_CHUNK):
        width = min(FF_CHUNK, d_ff - lo)
        gate = conv(jnp.dot(u, wu_ref[:, lo:lo + width], preferred_element_type=F32), lo, width)
        val = conv(jnp.dot(u, wu_ref[:, d_ff + lo:d_ff + lo + width], preferred_element_type=F32),
                   d_ff + lo, width)
        z = gate * _sigmoid(gate) * val
        part = _mm(z, wd_ref[lo:lo + width, :])
        acc = part if acc is None else acc + part
    y = x + mod_ref[5:6, :] * acc
    if final_norm:
        y = _rms(y, gf_ref[...])
    o_ref[...] = y


def _ffn(h, mod_l, mod_row0, wts, tm, g_final=None):
    nb, ns, d = h.shape
    d_ff = wts["w_down"].shape[0]
    per = tm // SUBLANES
    last = ns // SUBLANES - 1
    tok = pl.BlockSpec((None, tm, d), lambda b, i: (b, i, 0))
    in_specs = [
        tok,
        pl.BlockSpec((None, SUBLANES, d), lambda b, i: (b, jnp.maximum(i * per - 1, 0), 0)),
        pl.BlockSpec((None, SUBLANES, d), lambda b, i: (b, jnp.minimum((i + 1) * per, last), 0)),
        pl.BlockSpec((None, N_MOD, d), (lambda b, i: (b + 1, 0, 0)) if mod_row0 else (lambda b, i: (0, 0, 0))),
        _resident((1, d)), _resident((d, 2 * d_ff)), _resident((CONV_W, 2 * d_ff)), _resident((1, 2 * d_ff)),
        _resident((d_ff, d)),
    ]
    args = [h, h, h, mod_l, wts["g2"], wts["w_up"], wts["conv_w"], wts["conv_b"], wts["w_down"]]
    if g_final is not None:
        in_specs.append(_resident((1, d)))
        args.append(g_final)
    return pl.pallas_call(
        functools.partial(_ffn_kernel, final_norm=g_final is not None),
        out_shape=jax.ShapeDtypeStruct(h.shape, F32),
        grid=(nb, ns // tm),
        in_specs=in_specs,
        out_specs=tok,
        compiler_params=_params(2),
        name="conv_ffn",
    )(*args)


def _layer_weights(l, g_norm1, w_in, lam, g_diff, g_cq, w_uq, g_ckv, w_ukv, sink, w_branch, w_gate, b_gate,
                   w_o, g_norm2, w_up, conv_w, conv_b, w_down):
    mx = lambda a: a.astype(MXU_DTYPE)
    row = lambda a: a.reshape(1, -1).astype(F32)
    wb = w_branch[l]
    wb = jnp.stack([wb[0], wb[1], jnp.take(wb[2], jnp.asarray(_wbranch_c_src(), jnp.int32), axis=0)])
    return {
        "g1": row(g_norm1[l]), "w_in": mx(_take_cols(w_in[l], _in_proj_src())),
        "lam": lam[l].astype(F32), "g_diff": row(g_diff[l]),
        "g_cq": row(g_cq[l]), "w_uq": mx(_take_cols(w_uq[l], _wuq_src())),
        "g_ckv": row(g_ckv[l]), "w_kv": mx(_take_cols(w_ukv[l], _wkv_src())),
        "sink": sink[l].astype(F32),
        "w_gate": mx(w_gate[l]), "b_gate": row(b_gate[l]), "w_branch": mx(wb), "w_o": mx(w_o[l]),
        "g2": row(g_norm2[l]), "w_up": mx(w_up[l]), "conv_w": conv_w[l].astype(F32), "conv_b": row(conv_b[l]),
        "w_down": mx(w_down[l]),
    }


def _tile(n, want):
    t = min(n, want)
    assert n % t == 0, (n, t)
    return t


def kernel(x, c, ctx, c_ctx, w_ada, b_ada, g_norm1, w_in, lam, g_diff, g_cq, w_uq, g_ckv, w_ukv, sink, w_branch,
           w_gate, b_gate, w_o, g_norm2, w_up, conv_w, conv_b, w_down, g_final):
    n_batch, n_lat, d = x.shape
    n_ctx = ctx.shape[1]
    n_layers = w_ada.shape[0]
    assert n_lat % GRID_W == 0 and n_lat % WINDOW == 0
    tm_lat, tm_ctx = _tile(n_lat, 512), _tile(n_ctx, 256)
    tq_lat, tq_ctx = _tile(n_lat, 256), _tile(n_ctx, 256)
    assert n_lat >= tq_lat + 2 * WINDOW

    n_rows = -(-(1 + n_batch) // SUBLANES) * SUBLANES
    c_all = jnp.concatenate([c_ctx[None, :], c, jnp.zeros((n_rows - 1 - n_batch, d), c.dtype)], axis=0)
    mod = _modulation(c_all.astype(F32), w_ada, b_ada).reshape(n_layers, n_rows, N_MOD, d)
    tables = _rope_tables(n_lat)

    h, hc = x.astype(F32), ctx.astype(F32)
    for l in range(n_layers):
        ctx_out = l < n_layers - 1
        lam_init = 0.8 - 0.6 * math.exp(-0.3 * l)
        wts = _layer_weights(l, g_norm1, w_in, lam, g_diff, g_cq, w_uq, g_ckv, w_ukv, sink, w_branch, w_gate,
                             b_gate, w_o, g_norm2, w_up, conv_w, conv_b, w_down)
        lat = _inproj(h, mod[l], True, wts, tables, tm_lat, True)
        cx = _inproj(hc, mod[l], False, wts, None, tm_ctx, ctx_out)
        ys = _mixers(lat, cx, lat, wts, lam_init, tq_lat)
        h = _merge(h, mod[l], True, wts, ys, tm_lat)
        h = _ffn(h, mod[l], True, wts, tm_lat, g_final.reshape(1, d).astype(F32) if not ctx_out else None)
        if ctx_out:
            ys_c = _mixers(cx, cx, None, wts, lam_init, tq_ctx)
            hc = _merge(hc, mod[l], False, wts, ys_c, tm_ctx)
            hc = _ffn(hc, mod[l], False, wts, tm_ctx)
    return h
```

```python
import functools
import math

import numpy as np
import jax
import jax.numpy as jnp
from jax import lax
from jax.experimental import pallas as pl
from jax.experimental.pallas import tpu as pltpu

F32 = jnp.float32
MXU_DTYPE = jnp.bfloat16

GRID_W = 64
HA, DA = 4, 64
HB, NOPE_B, ROPE_B, V_B = 8, 64, 32, 64
Q_RANK, KV_RANK = 256, 128
HC, KVC, DC = 8, 2, 64
G_C = HC // KVC
WINDOW = 128
BRANCH_W = 512
N_BRANCH = 3
CONV_W = 3
ROPE_BASE = 10000.0
EPS = 1e-6
NEG_INF = -1e30
N_MOD = 6
MLA_SCALE = (NOPE_B + ROPE_B) ** -0.5
IN_WIDTHS = (HA * 2 * DA, HA * 2 * DA, HA * 2 * DA, Q_RANK, KV_RANK, ROPE_B, HC * DC, KVC * DC, KVC * DC)
IN_OFFS = tuple(int(v) for v in np.cumsum((0,) + IN_WIDTHS))

LANES = 128
SUBLANES = 8
HALF = LANES // 2
VMEM_LIMIT = 56 * 1024 * 1024

SEG_QA, SEG_KA, SEG_VA, SEG_QC = 0, 512, 1024, 1536
SEG_KC, SEG_VC, SEG_CQ, SEG_CKV, SEG_KR = 2048, 2176, 2432, 2688, 2816
NP_IN = 2944
LOG2E = 1.4426950408889634
KEY_BLOCK = 512
N_SLOTS = 2


def _slab64_src(base, unit_of_group):
    src = np.zeros(LANES, np.int64)
    for grp in range(4):
        unit, half = grp % 2, grp // 2
        for r in range(32):
            src[grp * 32 + r] = base + unit_of_group[unit] + half * 32 + r
    return src


def _in_proj_src():
    src = -np.ones(NP_IN, np.int64)
    qa0, ka0, va0, cq0, ckv0, kr0, qc0, kc0, vc0 = IN_OFFS[:9]
    for h in range(HA):
        src[SEG_QA + h * LANES:SEG_QA + (h + 1) * LANES] = _slab64_src(qa0 + h * 2 * DA, (0, DA))
        src[SEG_KA + h * LANES:SEG_KA + (h + 1) * LANES] = _slab64_src(ka0 + h * 2 * DA, (0, DA))
    src[SEG_VA:SEG_VA + 512] = va0 + np.arange(512)
    for g in range(G_C):
        src[SEG_QC + g * LANES:SEG_QC + (g + 1) * LANES] = _slab64_src(qc0, (g * DC, (G_C + g) * DC))
    src[SEG_KC:SEG_KC + LANES] = _slab64_src(kc0, (0, DC))
    for kv in range(KVC):
        v_lo = SEG_VC + kv * LANES + (kv % 2) * HALF
        src[v_lo:v_lo + DC] = vc0 + kv * DC + np.arange(DC)
    src[SEG_CQ:SEG_CQ + Q_RANK] = cq0 + np.arange(Q_RANK)
    src[SEG_CKV:SEG_CKV + KV_RANK] = ckv0 + np.arange(KV_RANK)
    src[SEG_KR:SEG_KR + 16] = kr0 + np.arange(16)
    src[SEG_KR + HALF:SEG_KR + HALF + 16] = kr0 + 16 + np.arange(16)
    return src


def _mla_slab_src(nope0, rope0):
    src = -np.ones(LANES, np.int64)
    if rope0 is not None:
        src[0:16] = rope0 + np.arange(16)
        src[HALF:HALF + 16] = rope0 + 16 + np.arange(16)
    src[16:64] = nope0 + np.arange(48)
    src[80:96] = nope0 + 48 + np.arange(16)
    return src


def _wuq_src():
    src = -np.ones(HB * LANES, np.int64)
    for h in range(HB):
        b0 = h * (NOPE_B + ROPE_B)
        src[h * LANES:(h + 1) * LANES] = _mla_slab_src(b0, b0 + NOPE_B)
    return src


def _wkv_src():
    src = -np.ones(2 * HB * LANES, np.int64)
    for h in range(HB):
        b0 = h * (NOPE_B + V_B)
        src[h * LANES:(h + 1) * LANES] = _mla_slab_src(b0, None)
        v_lo = HB * LANES + h * LANES + (h % 2) * HALF
        src[v_lo:v_lo + V_B] = b0 + NOPE_B + np.arange(V_B)
    return src


def _wbranch_c_src():
    src = np.zeros(BRANCH_W, np.int64)
    for g in range(G_C):
        for kv in range(KVC):
            src[g * LANES + kv * DC:g * LANES + (kv + 1) * DC] = (kv * G_C + g) * DC + np.arange(DC)
    return src


def _take_cols(w, src):
    cols = jnp.take(w, jnp.asarray(np.maximum(src, 0), jnp.int32), axis=-1)
    return jnp.where(jnp.asarray(src >= 0)[None, :], cols, 0.0)


def _rope_tables(n_tok):
    def table(rot_dim):
        n_freq = rot_dim // 4
        inv = ROPE_BASE ** (-jnp.arange(n_freq, dtype=F32) / n_freq)
        n_rows = n_tok // GRID_W
        rows = jnp.repeat(jnp.arange(n_rows, dtype=F32), GRID_W)
        cols = jnp.tile(jnp.arange(GRID_W, dtype=F32), n_rows)
        ang = jnp.concatenate([rows[:, None] * inv, cols[:, None] * inv], axis=-1)
        return jnp.cos(ang), jnp.sin(ang)

    cos64, sin64 = table(DA)
    cosb, sinb = table(ROPE_B)
    cos64_t = jnp.concatenate([cos64] * 4, axis=-1)
    sin64_t = jnp.concatenate([-sin64, -sin64, sin64, sin64], axis=-1)
    ones = jnp.ones((n_tok, HALF - 16), F32)
    zeros = jnp.zeros((n_tok, HALF - 16), F32)
    cosb_t = jnp.concatenate([cosb, ones, cosb, ones], axis=-1)
    sinb_t = jnp.concatenate([-sinb, zeros, sinb, zeros], axis=-1)
    return cos64_t, sin64_t, cosb_t, sinb_t


def _mm(a, b):
    return jnp.dot(a.astype(MXU_DTYPE), b.astype(MXU_DTYPE), preferred_element_type=F32)


def _mm_t(a, b):
    return lax.dot_general(a.astype(MXU_DTYPE), b.astype(MXU_DTYPE), (((1,), (1,)), ((), ())),
                           preferred_element_type=F32)


def _rms(x, g):
    return x * lax.rsqrt(jnp.mean(x * x, axis=-1, keepdims=True) + EPS) * g


def _norm_mod(x, g, shift, scale):
    return _rms(x, g) * (1.0 + scale) + shift


def _sigmoid(x):
    return 1.0 / (1.0 + jnp.exp(-x))


def _rope_slab(x, cos_t, sin_t):
    return x * cos_t + pltpu.roll(x, HALF, axis=1) * sin_t


def _ones_lane_of(slab):
    return HALF if slab % 2 == 0 else 0


def _ones_lanes(n_slab):
    lane = lax.broadcasted_iota(jnp.int32, (1, n_slab * LANES), 1)
    target = jnp.where((lane // LANES) % 2 == 0, _ones_lane_of(0), _ones_lane_of(1))
    return jnp.where(lane % LANES == target, 1.0, 0.0)


def _resident(shape):
    nd = len(shape)
    return pl.BlockSpec(shape, lambda *_: (0,) * nd, pipeline_mode=pl.Buffered(1))


def _params(n_grid):
    return pltpu.CompilerParams(dimension_semantics=("arbitrary",) * n_grid, vmem_limit_bytes=VMEM_LIMIT)


def _mod_kernel(c_ref, w_ref, b_ref, o_ref):
    c = c_ref[...]
    s = c * _sigmoid(c)
    o_ref[...] = _mm(s, w_ref[...]) + b_ref[...]


def _modulation(c_all, w_ada, b_ada):
    n_layers, d, n_out = w_ada.shape
    rows = c_all.shape[0]
    tn = 1536
    return pl.pallas_call(
        _mod_kernel,
        out_shape=jax.ShapeDtypeStruct((n_layers, rows, n_out), F32),
        grid=(n_layers, n_out // tn),
        in_specs=[
            pl.BlockSpec((rows, d), lambda l, j: (0, 0)),
            pl.BlockSpec((None, d, tn), lambda l, j: (l, 0, j)),
            pl.BlockSpec((None, 1, tn), lambda l, j: (l, 0, j)),
        ],
        out_specs=pl.BlockSpec((None, rows, tn), lambda l, j: (l, 0, j)),
        compiler_params=_params(2),
        name="modulation",
    )(c_all, w_ada, b_ada.reshape(n_layers, 1, n_out))


def _inproj_kernel(*refs, rope, need_q):
    it = iter(refs)
    h_ref, mod_ref, g1_ref, w_ref, gcq_ref, wuq_ref, gckv_ref, wkv_ref = (next(it) for _ in range(8))
    if rope:
        cos64_ref, sin64_ref, cosb_ref, sinb_ref = (next(it) for _ in range(4))
    if need_q:
        qa_ref, qb_ref, qc_ref = (next(it) for _ in range(3))
    ka_ref, va_ref, kb_ref, vb_ref, kc_ref, vc_ref = (next(it) for _ in range(6))

    u = _norm_mod(h_ref[...], g1_ref[...], mod_ref[0:1, :], mod_ref[1:2, :]).astype(MXU_DTYPE)

    def proj(lo, width):
        return jnp.dot(u, w_ref[:, lo:lo + width], preferred_element_type=F32)

    def rot64(x):
        return _rope_slab(x, cos64_ref[...], sin64_ref[...]) if rope else x

    def rotb(x):
        return _rope_slab(x, cosb_ref[...], sinb_ref[...]) if rope else x

    def store_rot64(dst_ref, lo, n_slab, scale):
        p = proj(lo, n_slab * LANES)
        for s in range(n_slab):
            y = rot64(p[:, s * LANES:(s + 1) * LANES])
            if scale != 1.0:
                y = y * scale
            dst_ref[:, s * LANES:(s + 1) * LANES] = y.astype(dst_ref.dtype)

    if need_q:
        store_rot64(qa_ref, SEG_QA, HA, DA ** -0.5 * LOG2E)
        store_rot64(qc_ref, SEG_QC, G_C, DC ** -0.5 * LOG2E)
    store_rot64(ka_ref, SEG_KA, HA, 1.0)
    va_ref[...] = proj(SEG_VA, 512).astype(va_ref.dtype)
    kcvc = proj(SEG_KC, (1 + KVC) * LANES)
    kc_ref[...] = rot64(kcvc[:, :LANES]).astype(kc_ref.dtype)
    vc_ref[...] = (kcvc[:, LANES:] + _ones_lanes(KVC)).astype(vc_ref.dtype)

    low = proj(SEG_CQ, Q_RANK + KV_RANK + LANES)
    if need_q:
        cq = _rms(low[:, :Q_RANK], gcq_ref[...])
        qh = _mm(cq, wuq_ref[...])
        for h in range(HB):
            y = rotb(qh[:, h * LANES:(h + 1) * LANES]) * (MLA_SCALE * LOG2E)
            qb_ref[:, h * LANES:(h + 1) * LANES] = y.astype(qb_ref.dtype)
    ckv = _rms(low[:, Q_RANK:Q_RANK + KV_RANK], gckv_ref[...])
    kvh = _mm(ckv, wkv_ref[...])
    kr = rotb(low[:, Q_RANK + KV_RANK:])
    for h in range(HB):
        kb_ref[:, h * LANES:(h + 1) * LANES] = (kvh[:, h * LANES:(h + 1) * LANES] + kr).astype(kb_ref.dtype)
    vb_ref[...] = (kvh[:, HB * LANES:] + _ones_lanes(HB)).astype(vb_ref.dtype)


def _inproj(h, mod_l, mod_row0, wts, tables, tm, need_q):
    nb, ns, d = h.shape
    rope = tables is not None
    tok = lambda w: pl.BlockSpec((None, tm, w), lambda b, i: (b, i, 0))
    in_specs = [
        tok(d),
        pl.BlockSpec((None, N_MOD, d), (lambda b, i: (b + 1, 0, 0)) if mod_row0 else (lambda b, i: (0, 0, 0))),
        _resident((1, d)), _resident((d, NP_IN)),
        _resident((1, Q_RANK)), _resident((Q_RANK, HB * LANES)),
        _resident((1, KV_RANK)), _resident((KV_RANK, 2 * HB * LANES)),
    ]
    args = [h, mod_l, wts["g1"], wts["w_in"], wts["g_cq"], wts["w_uq"], wts["g_ckv"], wts["w_kv"]]
    if rope:
        in_specs += [pl.BlockSpec((tm, LANES), lambda b, i: (i, 0))] * 4
        args += list(tables)
    names, widths = [], []
    if need_q:
        names += ["qa", "qb", "qc"]
        widths += [512, HB * LANES, 512]
    names += ["ka", "va", "kb", "vb", "kc", "vc"]
    widths += [512, 512, HB * LANES, HB * LANES, LANES, KVC * LANES]
    outs = pl.pallas_call(
        functools.partial(_inproj_kernel, rope=rope, need_q=need_q),
        out_shape=[jax.ShapeDtypeStruct((nb, ns, w), MXU_DTYPE) for w in widths],
        grid=(nb, ns // tm),
        in_specs=in_specs,
        out_specs=[tok(w) for w in widths],
        compiler_params=_params(2),
        name="inproj",
    )(*args)
    return dict(zip(names, outs))


def _score_pass(q, kv_blocks, s_ref, m_ref, m_extra=None):
    n_rows = q.shape[0]
    col = 0
    for idx, (load_k, _, ok, n) in enumerate(kv_blocks):
        s = _mm_t(q, load_k())
        if ok is not None:
            s = jnp.where(ok, s, NEG_INF)
        s_ref[:, col:col + n] = s
        block_max = functools.reduce(jnp.maximum, [s[:, c * LANES:(c + 1) * LANES] for c in range(n // LANES)])
        m_ref[...] = block_max if idx == 0 else jnp.maximum(m_ref[...], block_max)
        col += n
        yield
    m = jnp.max(m_ref[...], axis=-1, keepdims=True)
    if m_extra is not None:
        m = jnp.maximum(m, m_extra)
    m_ref[...] = jnp.broadcast_to(m, (n_rows, LANES))
    return m


def _value_pass(kv_blocks, s_ref, m_ref, l_ref=None):
    col, o = 0, None
    for idx, (_, load_v, _, n) in enumerate(kv_blocks):
        ps = [jnp.exp2(s_ref[:, col + c * LANES:col + (c + 1) * LANES] - m_ref[...]) for c in range(n // LANES)]
        if l_ref is not None:
            block_sum = functools.reduce(lambda a, b: a + b, ps)
            l_ref[...] = block_sum if idx == 0 else l_ref[...] + block_sum
        pv = _mm(jnp.concatenate(ps, axis=-1), load_v())
        o = pv if o is None else o + pv
        col += n
        yield
    l = jnp.sum(l_ref[...], axis=-1, keepdims=True) if l_ref is not None else None
    return o, l


def _step_together(gens):
    results, live = [None] * len(gens), list(range(len(gens)))
    while live:
        for i in list(live):
            try:
                next(gens[i])
            except StopIteration as stop:
                results[i] = stop.value
                live.remove(i)
    return results


def _run_units(units):
    pending = None
    for score_fn, finish_fn in units:
        gens = [score_fn()] + ([pending] if pending is not None else [])
        pending = finish_fn(_step_together(gens)[0])
    _step_together([pending])


def _key_blocks(kc_ref, vc_ref, kl_ref, vl_ref, k_sl, v_sl):
    blocks = [(lambda: kc_ref[:, k_sl], lambda: vc_ref[:, v_sl], None, kc_ref.shape[0])]
    if kl_ref is not None:
        n_lat = kl_ref.shape[0]
        kb = min(KEY_BLOCK, n_lat)
        for j in range(n_lat // kb):
            rows = slice(j * kb, (j + 1) * kb)
            blocks.append((lambda rows=rows: kl_ref[rows, k_sl], lambda rows=rows: vl_ref[rows, v_sl], None, kb))
    return blocks


def _unit_mask(unit):
    lane = lax.broadcasted_iota(jnp.int32, (1, LANES), 1)
    return jnp.where((lane // 32) % 2 == unit, 1.0, 0.0).astype(MXU_DTYPE)


def _attn_a_kernel(*refs, lam_init, has_lat):
    if has_lat:
        lam_ref, gd_ref, q_ref, kc_ref, vc_ref, kl_ref, vl_ref, o_ref, s_ref, m_ref, l_ref = refs
    else:
        lam_ref, gd_ref, q_ref, kc_ref, vc_ref, o_ref, s_ref, m_ref, l_ref = refs
        kl_ref = vl_ref = None
    tq = q_ref.shape[0]
    lp = lam_ref[...]
    lam = (jnp.exp(jnp.sum(lp[0:1] * lp[1:2], axis=-1, keepdims=True))
           - jnp.exp(jnp.sum(lp[2:3] * lp[3:4], axis=-1, keepdims=True)) + lam_init)
    m0, m1 = _unit_mask(0), _unit_mask(1)

    def unit(h):
        sl = slice(h * LANES, (h + 1) * LANES)
        slot = h % N_SLOTS
        blocks = _key_blocks(kc_ref, vc_ref, kl_ref, vl_ref, sl, sl)
        s_h, m_h, l_h = s_ref.at[slot], m_ref.at[slot], l_ref.at[slot]

        def score():
            q = q_ref[:, sl]
            qq = jnp.concatenate([q * m0, q * m1], axis=0)
            return (yield from _score_pass(qq, blocks, s_h, m_h))

        def finish(_):
            o2, l2 = yield from _value_pass(blocks, s_h, m_h, l_h)
            r = 1.0 / l2
            o = o2[:tq] * r[:tq] - o2[tq:] * (lam * r[tq:])
            o = _rms(o, gd_ref[...]) * (1.0 - lam_init)
            o_ref[:, sl] = o.astype(o_ref.dtype)

        return score, finish

    _run_units([unit(h) for h in range(HA)])


def _attn_b_kernel(*refs, has_lat):
    if has_lat:
        q_ref, kc_ref, vc_ref, kl_ref, vl_ref, o_ref, s_ref, m_ref = refs
    else:
        q_ref, kc_ref, vc_ref, o_ref, s_ref, m_ref = refs
        kl_ref = vl_ref = None
    lane = lax.broadcasted_iota(jnp.int32, (1, LANES), 1)
    o_even = {}

    def unit(h):
        sl = slice(h * LANES, (h + 1) * LANES)
        slot = h % N_SLOTS
        blocks = _key_blocks(kc_ref, vc_ref, kl_ref, vl_ref, sl, sl)
        s_h, m_h = s_ref.at[slot], m_ref.at[slot]

        def finish(_):
            o, _ = yield from _value_pass(blocks, s_h, m_h)
            one = _ones_lane_of(h)
            o = o * (1.0 / o[:, one:one + 1])
            if h % 2 == 0:
                o_even[h // 2] = o
            else:
                pair = h // 2
                o_ref[:, pair * LANES:(pair + 1) * LANES] = jnp.where(lane < HALF, o_even.pop(pair), o).astype(o_ref.dtype)

        return (lambda: _score_pass(q_ref[:, sl], blocks, s_h, m_h)), finish

    _run_units([unit(h) for h in range(HB)])


def _attn_c_kernel(*refs, has_lat, n_lat):
    if has_lat:
        sink_ref, q_ref, kc_ref, vc_ref, kl_ref, vl_ref, o_ref, s_ref, m_ref = refs
    else:
        sink_ref, q_ref, kc_ref, vc_ref, o_ref, s_ref, m_ref = refs
    tq = q_ref.shape[0]
    if has_lat:
        band = tq + 2 * WINDOW
        q0 = pl.program_id(1) * tq
        start = pl.multiple_of(jnp.minimum(jnp.maximum(q0 - WINDOW, 0), n_lat - band), WINDOW)
        qpos = q0 + lax.broadcasted_iota(jnp.int32, (tq, 1), 0)
        kpos = start + lax.broadcasted_iota(jnp.int32, (1, band), 1)
        ok = jnp.abs(qpos - kpos) <= WINDOW
    lane = lax.broadcasted_iota(jnp.int32, (1, LANES), 1)
    masks = (_unit_mask(0), _unit_mask(1))
    o_kv0 = {}

    def unit(g, kv):
        sl = slice(g * LANES, (g + 1) * LANES)
        v_sl = slice(kv * LANES, (kv + 1) * LANES)
        blocks = [(lambda: kc_ref[...], lambda: vc_ref[:, v_sl], None, kc_ref.shape[0])]
        if has_lat:
            blocks.append((lambda: kl_ref[pl.ds(start, band), :], lambda: vl_ref[pl.ds(start, band), v_sl], ok, band))
        slot = (g * KVC + kv) % N_SLOTS
        s_u, m_u = s_ref.at[slot], m_ref.at[slot]

        def score():
            sink = jnp.full((1, 1), sink_ref[kv * G_C + g] * LOG2E, F32)
            m = yield from _score_pass(q_ref[:, sl] * masks[kv], blocks, s_u, m_u, m_extra=sink)
            return jnp.exp2(sink - m)

        def finish(sink_term):
            o, _ = yield from _value_pass(blocks, s_u, m_u)
            one = _ones_lane_of(kv)
            o = o * (1.0 / (o[:, one:one + 1] + sink_term))
            if kv == 0:
                o_kv0[g] = o
            else:
                o_ref[:, sl] = jnp.where(lane < HALF, o_kv0.pop(g), o).astype(o_ref.dtype)

        return score, finish

    _run_units([unit(g, kv) for g in range(G_C) for kv in range(KVC)])


def _attention(kind, q, kv_ctx, kv_lat, tq, extra_args, extra_specs, **static):
    nb, nq, qw = q.shape
    has_lat = kv_lat is not None
    full = lambda a: pl.BlockSpec((None,) + a.shape[1:], lambda b, i: (b, 0, 0))
    in_specs = list(extra_specs) + [pl.BlockSpec((None, tq, qw), lambda b, i: (b, i, 0))]
    in_specs += [full(kv_ctx[0]), full(kv_ctx[1])]
    args = list(extra_args) + [q, kv_ctx[0], kv_ctx[1]]
    n_keys = kv_ctx[0].shape[1]
    if has_lat:
        in_specs += [full(kv_lat[0]), full(kv_lat[1])]
        args += [kv_lat[0], kv_lat[1]]
        n_keys += (tq + 2 * WINDOW) if kind == "c" else kv_lat[0].shape[1]
    n_rows = 2 * tq if kind == "a" else tq
    scratch = [pltpu.VMEM((N_SLOTS, n_rows, n_keys), F32), pltpu.VMEM((N_SLOTS, n_rows, LANES), F32)]
    if kind == "a":
        scratch.append(pltpu.VMEM((N_SLOTS, n_rows, LANES), F32))
    body = {"a": _attn_a_kernel, "b": _attn_b_kernel, "c": _attn_c_kernel}[kind]
    return pl.pallas_call(
        functools.partial(body, has_lat=has_lat, **static),
        out_shape=jax.ShapeDtypeStruct((nb, nq, BRANCH_W), MXU_DTYPE),
        grid=(nb, nq // tq),
        in_specs=in_specs,
        out_specs=pl.BlockSpec((None, tq, BRANCH_W), lambda b, i: (b, i, 0)),
        scratch_shapes=scratch,
        compiler_params=_params(2),
        name="attn_" + kind,
    )(*args)


def _mixers(qs, ctx_kv, lat_kv, wts, lam_init, tq):
    has_lat = lat_kv is not None
    pick = lambda d, k, v: (d[k], d[v])
    ya = _attention("a", qs["qa"], pick(ctx_kv, "ka", "va"), pick(lat_kv, "ka", "va") if has_lat else None, tq,
                    [wts["lam"], wts["g_diff"]], [_resident((4, DA)), _resident((1, 2 * DA))], lam_init=lam_init)
    yb = _attention("b", qs["qb"], pick(ctx_kv, "kb", "vb"), pick(lat_kv, "kb", "vb") if has_lat else None, tq,
                    [], [])
    n_lat = lat_kv["kc"].shape[1] if has_lat else 0
    yc = _attention("c", qs["qc"], pick(ctx_kv, "kc", "vc"), pick(lat_kv, "kc", "vc") if has_lat else None, tq,
                    [wts["sink"]], [pl.BlockSpec(memory_space=pltpu.SMEM)], n_lat=n_lat)
    return ya, yb, yc


def _merge_kernel(h_ref, mod_ref, g1_ref, ya_ref, yb_ref, yc_ref, wg_ref, bg_ref, wb_ref, wo_ref, o_ref):
    x = h_ref[...]
    d = x.shape[-1]
    u = _norm_mod(x, g1_ref[...], mod_ref[0:1, :], mod_ref[1:2, :]).astype(MXU_DTYPE)
    acc = None
    for n, y_ref in enumerate((ya_ref, yb_ref, yc_ref)):
        gate = _sigmoid(jnp.dot(u, wg_ref[:, n * d:(n + 1) * d], preferred_element_type=F32)
                        + bg_ref[:, n * d:(n + 1) * d])
        term = gate * jnp.dot(y_ref[...], wb_ref[n], preferred_element_type=F32)
        acc = term if acc is None else acc + term
    o_ref[...] = x + mod_ref[2:3, :] * _mm(acc, wo_ref[...])


def _merge(h, mod_l, mod_row0, wts, ys, tm):
    nb, ns, d = h.shape
    tok = lambda w: pl.BlockSpec((None, tm, w), lambda b, i: (b, i, 0))
    return pl.pallas_call(
        _merge_kernel,
        out_shape=jax.ShapeDtypeStruct(h.shape, F32),
        grid=(nb, ns // tm),
        in_specs=[
            tok(d),
            pl.BlockSpec((None, N_MOD, d), (lambda b, i: (b + 1, 0, 0)) if mod_row0 else (lambda b, i: (0, 0, 0))),
            _resident((1, d)), tok(BRANCH_W), tok(BRANCH_W), tok(BRANCH_W),
            _resident((d, N_BRANCH * d)), _resident((1, N_BRANCH * d)),
            _resident((N_BRANCH, BRANCH_W, d)), _resident((d, d)),
        ],
        out_specs=tok(d),
        compiler_params=_params(2),
        name="merge",
    )(h, mod_l, wts["g1"], *ys, wts["w_gate"], wts["b_gate"], wts["w_branch"], wts["w_o"])


FF_CHUNK = 512


def _ffn_kernel(*refs, final_norm):
    if final_norm:
        h_ref, hp_ref, hn_ref, mod_ref, g2_ref, wu_ref, cw_ref, cb_ref, wd_ref, gf_ref, o_ref = refs
    else:
        h_ref, hp_ref, hn_ref, mod_ref, g2_ref, wu_ref, cw_ref, cb_ref, wd_ref, o_ref = refs
    i, n_tiles = pl.program_id(1), pl.num_programs(1)
    x = h_ref[...]
    tm, d = x.shape
    d_ff = wd_ref.shape[0]
    shift, scale, g2 = mod_ref[3:4, :], mod_ref[4:5, :], g2_ref[...]
    u_prev = jnp.where(i > 0, _norm_mod(hp_ref[...], g2, shift, scale), 0.0)
    u_next = jnp.where(i < n_tiles - 1, _norm_mod(hn_ref[...], g2, shift, scale), 0.0)
    u = jnp.concatenate([u_prev, _norm_mod(x, g2, shift, scale), u_next], axis=0).astype(MXU_DTYPE)
    n_ext = tm + 2 * SUBLANES

    def conv(a, lo, width):
        w = cw_ref[:, lo:lo + width]
        prev = pltpu.roll(a, 1, axis=0)[SUBLANES:SUBLANES + tm]
        nxt = pltpu.roll(a, n_ext - 1, axis=0)[SUBLANES:SUBLANES + tm]
        return (w[0:1] * prev + w[1:2] * a[SUBLANES:SUBLANES + tm] + w[2:3] * nxt) + cb_ref[:, lo:lo + width]

    acc = None
    for lo in range(0, d_ff, FF_CHUNK):
        width = min(FF_CHUNK, d_ff - lo)
        gate = conv(jnp.dot(u, wu_ref[:, lo:lo + width], preferred_element_type=F32), lo, width)
        val = conv(jnp.dot(u, wu_ref[:, d_ff + lo:d_ff + lo + width], preferred_element_type=F32),
                   d_ff + lo, width)
        z = gate * _sigmoid(gate) * val
        part = _mm(z, wd_ref[lo:lo + width, :])
        acc = part if acc is None else acc + part
    y = x + mod_ref[5:6, :] * acc
    if final_norm:
        y = _rms(y, gf_ref[...])
    o_ref[...] = y


def _ffn(h, mod_l, mod_row0, wts, tm, g_final=None):
    nb, ns, d = h.shape
    d_ff = wts["w_down"].shape[0]
    per = tm // SUBLANES
    last = ns // SUBLANES - 1
    tok = pl.BlockSpec((None, tm, d), lambda b, i: (b, i, 0))
    in_specs = [
        tok,
        pl.BlockSpec((None, SUBLANES, d), lambda b, i: (b, jnp.maximum(i * per - 1, 0), 0)),
        pl.BlockSpec((None, SUBLANES, d), lambda b, i: (b, jnp.minimum((i + 1) * per, last), 0)),
        pl.BlockSpec((None, N_MOD, d), (lambda b, i: (b + 1, 0, 0)) if mod_row0 else (lambda b, i: (0, 0, 0))),
        _resident((1, d)), _resident((d, 2 * d_ff)), _resident((CONV_W, 2 * d_ff)), _resident((1, 2 * d_ff)),
        _resident((d_ff, d)),
    ]
    args = [h, h, h, mod_l, wts["g2"], wts["w_up"], wts["conv_w"], wts["conv_b"], wts["w_down"]]
    if g_final is not None:
        in_specs.append(_resident((1, d)))
        args.append(g_final)
    return pl.pallas_call(
        functools.partial(_ffn_kernel, final_norm=g_final is not None),
        out_shape=jax.ShapeDtypeStruct(h.shape, F32),
        grid=(nb, ns // tm),
        in_specs=in_specs,
        out_specs=tok,
        compiler_params=_params(2),
        name="conv_ffn",
    )(*args)


def _layer_weights(l, g_norm1, w_in, lam, g_diff, g_cq, w_uq, g_ckv, w_ukv, sink, w_branch, w_gate, b_gate,
                   w_o, g_norm2, w_up, conv_w, conv_b, w_down):
    mx = lambda a: a.astype(MXU_DTYPE)
    row = lambda a: a.reshape(1, -1).astype(F32)
    wb = w_branch[l]
    wb = jnp.stack([wb[0], wb[1], jnp.take(wb[2], jnp.asarray(_wbranch_c_src(), jnp.int32), axis=0)])
    return {
        "g1": row(g_norm1[l]), "w_in": mx(_take_cols(w_in[l], _in_proj_src())),
        "lam": lam[l].astype(F32), "g_diff": row(g_diff[l]),
        "g_cq": row(g_cq[l]), "w_uq": mx(_take_cols(w_uq[l], _wuq_src())),
        "g_ckv": row(g_ckv[l]), "w_kv": mx(_take_cols(w_ukv[l], _wkv_src())),
        "sink": sink[l].astype(F32),
        "w_gate": mx(w_gate[l]), "b_gate": row(b_gate[l]), "w_branch": mx(wb), "w_o": mx(w_o[l]),
        "g2": row(g_norm2[l]), "w_up": mx(w_up[l]), "conv_w": conv_w[l].astype(F32), "conv_b": row(conv_b[l]),
        "w_down": mx(w_down[l]),
    }


def _tile(n, want):
    t = min(n, want)
    assert n % t == 0, (n, t)
    return t


def kernel(x, c, ctx, c_ctx, w_ada, b_ada, g_norm1, w_in, lam, g_diff, g_cq, w_uq, g_ckv, w_ukv, sink, w_branch,
           w_gate, b_gate, w_o, g_norm2, w_up, conv_w, conv_b, w_down, g_final):
    n_batch, n_lat, d = x.shape
    n_ctx = ctx.shape[1]
    n_layers = w_ada.shape[0]
    assert n_lat % GRID_W == 0 and n_lat % WINDOW == 0
    tm_lat, tm_ctx = _tile(n_lat, 512), _tile(n_ctx, 256)
    tq_lat, tq_ctx = _tile(n_lat, 256), _tile(n_ctx, 256)
    assert n_lat >= tq_lat + 2 * WINDOW

    n_rows = -(-(1 + n_batch) // SUBLANES) * SUBLANES
    c_all = jnp.concatenate([c_ctx[None, :], c, jnp.zeros((n_rows - 1 - n_batch, d), c.dtype)], axis=0)
    mod = _modulation(c_all.astype(F32), w_ada, b_ada).reshape(n_layers, n_rows, N_MOD, d)
    tables = _rope_tables(n_lat)

    h, hc = x.astype(F32), ctx.astype(F32)
    for l in range(n_layers):
        ctx_out = l < n_layers - 1
        lam_init = 0.8 - 0.6 * math.exp(-0.3 * l)
        wts = _layer_weights(l, g_norm1, w_in, lam, g_diff, g_cq, w_uq, g_ckv, w_ukv, sink, w_branch, w_gate,
                             b_gate, w_o, g_norm2, w_up, conv_w, conv_b, w_down)
        lat = _inproj(h, mod[l], True, wts, tables, tm_lat, True)
        cx = _inproj(hc, mod[l], False, wts, None, tm_ctx, ctx_out)
        ys = _mixers(lat, cx, lat, wts, lam_init, tq_lat)
        h = _merge(h, mod[l], True, wts, ys, tm_lat)
        h = _ffn(h, mod[l], True, wts, tm_lat, g_final.reshape(1, d).astype(F32) if not ctx_out else None)
        if ctx_out:
            ys_c = _mixers(cx, cx, None, wts, lam_init, tq_ctx)
            hc = _merge(hc, mod[l], False, wts, ys_c, tm_ctx)
            hc = _ffn(hc, mod[l], False, wts, tm_ctx)
    return h
```

```python
import functools
import math

import numpy as np
import jax
import jax.numpy as jnp
from jax import lax
from jax.experimental import pallas as pl
from jax.experimental.pallas import tpu as pltpu

F32 = jnp.float32
MXU_DTYPE = jnp.bfloat16

GRID_W = 64
HA, DA = 4, 64
HB, NOPE_B, ROPE_B, V_B = 8, 64, 32, 64
Q_RANK, KV_RANK = 256, 128
HC, KVC, DC = 8, 2, 64
G_C = HC // KVC
WINDOW = 128
BRANCH_W = 512
N_BRANCH = 3
CONV_W = 3
ROPE_BASE = 10000.0
EPS = 1e-6
NEG_INF = -1e30
N_MOD = 6
MLA_SCALE = (NOPE_B + ROPE_B) ** -0.5
IN_WIDTHS = (HA * 2 * DA, HA * 2 * DA, HA * 2 * DA, Q_RANK, KV_RANK, ROPE_B, HC * DC, KVC * DC, KVC * DC)
IN_OFFS = tuple(int(v) for v in np.cumsum((0,) + IN_WIDTHS))

LANES = 128
SUBLANES = 8
HALF = LANES // 2
VMEM_LIMIT = 56 * 1024 * 1024

SEG_QA, SEG_KA, SEG_VA, SEG_QC = 0, 512, 1024, 1536
SEG_KC, SEG_VC, SEG_CQ, SEG_CKV, SEG_KR = 2048, 2176, 2432, 2688, 2816
NP_IN = 2944
LOG2E = 1.4426950408889634
KEY_BLOCK = 512
N_SLOTS = 2

def _slab64_src(base, unit_of_group):
    src = np.zeros(LANES, np.int64)
    for grp in range(4):
        unit, half = grp % 2, grp // 2
        for r in range(32):
            src[grp * 32 + r] = base + unit_of_group[unit] + half * 32 + r
    return src


def _in_proj_src():
    src = -np.ones(NP_IN, np.int64)
    qa0, ka0, va0, cq0, ckv0, kr0, qc0, kc0, vc0 = IN_OFFS[:9]
    for h in range(HA):
        src[SEG_QA + h * LANES:SEG_QA + (h + 1) * LANES] = _slab64_src(qa0 + h * 2 * DA, (0, DA))
        src[SEG_KA + h * LANES:SEG_KA + (h + 1) * LANES] = _slab64_src(ka0 + h * 2 * DA, (0, DA))
    src[SEG_VA:SEG_VA + 512] = va0 + np.arange(512)
    for g in range(G_C):
        src[SEG_QC + g * LANES:SEG_QC + (g + 1) * LANES] = _slab64_src(qc0, (g * DC, (G_C + g) * DC))
    src[SEG_KC:SEG_KC + LANES] = _slab64_src(kc0, (0, DC))
    for kv in range(KVC):
        v_lo = SEG_VC + kv * LANES + (kv % 2) * HALF
        src[v_lo:v_lo + DC] = vc0 + kv * DC + np.arange(DC)
    src[SEG_CQ:SEG_CQ + Q_RANK] = cq0 + np.arange(Q_RANK)
    src[SEG_CKV:SEG_CKV + KV_RANK] = ckv0 + np.arange(KV_RANK)
    src[SEG_KR:SEG_KR + 16] = kr0 + np.arange(16)
    src[SEG_KR + HALF:SEG_KR + HALF + 16] = kr0 + 16 + np.arange(16)
    return src


def _mla_slab_src(nope0, rope0):
    src = -np.ones(LANES, np.int64)
    if rope0 is not None:
        src[0:16] = rope0 + np.arange(16)
        src[HALF:HALF + 16] = rope0 + 16 + np.arange(16)
    src[16:64] = nope0 + np.arange(48)
    src[80:96] = nope0 + 48 + np.arange(16)
    return src


def _wuq_src():
    src = -np.ones(HB * LANES, np.int64)
    for h in range(HB):
        b0 = h * (NOPE_B + ROPE_B)
        src[h * LANES:(h + 1) * LANES] = _mla_slab_src(b0, b0 + NOPE_B)
    return src


def _wkv_src():
    src = -np.ones(2 * HB * LANES, np.int64)
    for h in range(HB):
        b0 = h * (NOPE_B + V_B)
        src[h * LANES:(h + 1) * LANES] = _mla_slab_src(b0, None)
        v_lo = HB * LANES + h * LANES + (h % 2) * HALF
        src[v_lo:v_lo + V_B] = b0 + NOPE_B + np.arange(V_B)
    return src


def _wbranch_c_src():
    src = np.zeros(BRANCH_W, np.int64)
    for g in range(G_C):
        for kv in range(KVC):
            src[g * LANES + kv * DC:g * LANES + (kv + 1) * DC] = (kv * G_C + g) * DC + np.arange(DC)
    return src


def _take_cols(w, src):
    parts, start = [], 0
    for i in range(1, len(src) + 1):
        if i == len(src) or (src[i] != src[i - 1] + 1 if src[i - 1] >= 0 else src[i] >= 0):
            if src[start] < 0:
                parts.append(jnp.zeros(w.shape[:-1] + (i - start,), w.dtype))
            else:
                parts.append(w[..., int(src[start]):int(src[start]) + i - start])
            start = i
    return jnp.concatenate(parts, axis=-1)


def _rope_tables(n_tok):
    def table(rot_dim):
        n_freq = rot_dim // 4
        inv = ROPE_BASE ** (-jnp.arange(n_freq, dtype=F32) / n_freq)
        n_rows = n_tok // GRID_W
        rows = jnp.repeat(jnp.arange(n_rows, dtype=F32), GRID_W)
        cols = jnp.tile(jnp.arange(GRID_W, dtype=F32), n_rows)
        ang = jnp.concatenate([rows[:, None] * inv, cols[:, None] * inv], axis=-1)
        return jnp.cos(ang), jnp.sin(ang)

    cos64, sin64 = table(DA)
    cosb, sinb = table(ROPE_B)
    cos64_t = jnp.concatenate([cos64] * 4, axis=-1)
    sin64_t = jnp.concatenate([-sin64, -sin64, sin64, sin64], axis=-1)
    ones = jnp.ones((n_tok, HALF - 16), F32)
    zeros = jnp.zeros((n_tok, HALF - 16), F32)
    cosb_t = jnp.concatenate([cosb, ones, cosb, ones], axis=-1)
    sinb_t = jnp.concatenate([-sinb, zeros, sinb, zeros], axis=-1)
    return cos64_t, sin64_t, cosb_t, sinb_t


def _mm(a, b):
    return jnp.dot(a.astype(MXU_DTYPE), b.astype(MXU_DTYPE), preferred_element_type=F32)


def _mm_t(a, b):
    return lax.dot_general(a.astype(MXU_DTYPE), b.astype(MXU_DTYPE), (((1,), (1,)), ((), ())),
                           preferred_element_type=F32)


def _rms(x, g):
    return x * lax.rsqrt(jnp.mean(x * x, axis=-1, keepdims=True) + EPS) * g


def _norm_mod(x, g, shift, scale):
    return _rms(x, g) * (1.0 + scale) + shift


def _sigmoid(x):
    return 1.0 / (1.0 + jnp.exp(-x))


def _rope_slab(x, cos_t, sin_t):
    return x * cos_t + pltpu.roll(x, HALF, axis=1) * sin_t


def _ones_lane_of(slab):
    return HALF if slab % 2 == 0 else 0


def _ones_lanes(n_slab):
    lane = lax.broadcasted_iota(jnp.int32, (1, n_slab * LANES), 1)
    target = jnp.where((lane // LANES) % 2 == 0, _ones_lane_of(0), _ones_lane_of(1))
    return jnp.where(lane % LANES == target, 1.0, 0.0)


def _resident(shape):
    nd = len(shape)
    return pl.BlockSpec(shape, lambda *_: (0,) * nd, pipeline_mode=pl.Buffered(1))


def _params(n_grid):
    return pltpu.CompilerParams(dimension_semantics=("arbitrary",) * n_grid, vmem_limit_bytes=VMEM_LIMIT)


def _mod_kernel(c_ref, w_ref, b_ref, o_ref):
    c = c_ref[...]
    s = c * _sigmoid(c)
    o_ref[...] = _mm(s, w_ref[...]) + b_ref[...]


def _modulation(c_all, w_ada, b_ada):
    n_layers, d, n_out = w_ada.shape
    rows = c_all.shape[0]
    tn = 1536
    return pl.pallas_call(
        _mod_kernel,
        out_shape=jax.ShapeDtypeStruct((n_layers, rows, n_out), F32),
        grid=(n_layers, n_out // tn),
        in_specs=[
            pl.BlockSpec((rows, d), lambda l, j: (0, 0)),
            pl.BlockSpec((None, d, tn), lambda l, j: (l, 0, j)),
            pl.BlockSpec((None, 1, tn), lambda l, j: (l, 0, j)),
        ],
        out_specs=pl.BlockSpec((None, rows, tn), lambda l, j: (l, 0, j)),
        compiler_params=_params(2),
        name="modulation",
    )(c_all, w_ada, b_ada.reshape(n_layers, 1, n_out))


def _inproj_kernel(*refs, rope, need_q):
    it = iter(refs)
    h_ref, mod_ref, g1_ref, w_ref, gcq_ref, wuq_ref, gckv_ref, wkv_ref = (next(it) for _ in range(8))
    if rope:
        cos64_ref, sin64_ref, cosb_ref, sinb_ref = (next(it) for _ in range(4))
    if need_q:
        qa_ref, qb_ref, qc_ref = (next(it) for _ in range(3))
    ka_ref, va_ref, kb_ref, vb_ref, kc_ref, vc_ref = (next(it) for _ in range(6))

    u = _norm_mod(h_ref[...], g1_ref[...], mod_ref[0:1, :], mod_ref[1:2, :]).astype(MXU_DTYPE)

    def proj(lo, width):
        return jnp.dot(u, w_ref[:, lo:lo + width], preferred_element_type=F32)

    def rot64(x):
        return _rope_slab(x, cos64_ref[...], sin64_ref[...]) if rope else x

    def rotb(x):
        return _rope_slab(x, cosb_ref[...], sinb_ref[...]) if rope else x

    def store_rot64(dst_ref, lo, n_slab, scale):
        p = proj(lo, n_slab * LANES)
        for s in range(n_slab):
            y = rot64(p[:, s * LANES:(s + 1) * LANES])
            if scale != 1.0:
                y = y * scale
            dst_ref[:, s * LANES:(s + 1) * LANES] = y.astype(dst_ref.dtype)

    low = proj(SEG_CQ, Q_RANK + KV_RANK + LANES)
    if need_q:
        store_rot64(qa_ref, SEG_QA, HA, DA ** -0.5 * LOG2E)
        store_rot64(qc_ref, SEG_QC, G_C, DC ** -0.5 * LOG2E)
    store_rot64(ka_ref, SEG_KA, HA, 1.0)
    va_ref[...] = proj(SEG_VA, 512).astype(va_ref.dtype)
    kcvc = proj(SEG_KC, (1 + KVC) * LANES)
    kc_ref[...] = rot64(kcvc[:, :LANES]).astype(kc_ref.dtype)
    vc_ref[...] = (kcvc[:, LANES:] + _ones_lanes(KVC)).astype(vc_ref.dtype)

    if need_q:
        cq = _rms(low[:, :Q_RANK], gcq_ref[...])
        qh = _mm(cq, wuq_ref[...])
        for h in range(HB):
            y = rotb(qh[:, h * LANES:(h + 1) * LANES]) * (MLA_SCALE * LOG2E)
            qb_ref[:, h * LANES:(h + 1) * LANES] = y.astype(qb_ref.dtype)
    ckv = _rms(low[:, Q_RANK:Q_RANK + KV_RANK], gckv_ref[...])
    kvh = _mm(ckv, wkv_ref[...])
    kr = rotb(low[:, Q_RANK + KV_RANK:])
    for h in range(HB):
        kb_ref[:, h * LANES:(h + 1) * LANES] = (kvh[:, h * LANES:(h + 1) * LANES] + kr).astype(kb_ref.dtype)
    vb_ref[...] = (kvh[:, HB * LANES:] + _ones_lanes(HB)).astype(vb_ref.dtype)


def _inproj(h, mod_l, mod_row0, wts, tables, tm, need_q):
    nb, ns, d = h.shape
    rope = tables is not None
    tok = lambda w: pl.BlockSpec((None, tm, w), lambda b, i: (b, i, 0))
    in_specs = [
        tok(d),
        pl.BlockSpec((None, N_MOD, d), (lambda b, i: (b + 1, 0, 0)) if mod_row0 else (lambda b, i: (0, 0, 0))),
        _resident((1, d)), _resident((d, NP_IN)),
        _resident((1, Q_RANK)), _resident((Q_RANK, HB * LANES)),
        _resident((1, KV_RANK)), _resident((KV_RANK, 2 * HB * LANES)),
    ]
    args = [h, mod_l, wts["g1"], wts["w_in"], wts["g_cq"], wts["w_uq"], wts["g_ckv"], wts["w_kv"]]
    if rope:
        in_specs += [pl.BlockSpec((tm, LANES), lambda b, i: (i, 0))] * 4
        args += list(tables)
    names, widths = [], []
    if need_q:
        names += ["qa", "qb", "qc"]
        widths += [512, HB * LANES, 512]
    names += ["ka", "va", "kb", "vb", "kc", "vc"]
    widths += [512, 512, HB * LANES, HB * LANES, LANES, KVC * LANES]
    outs = pl.pallas_call(
        functools.partial(_inproj_kernel, rope=rope, need_q=need_q),
        out_shape=[jax.ShapeDtypeStruct((nb, ns, w), MXU_DTYPE) for w in widths],
        grid=(nb, ns // tm),
        in_specs=in_specs,
        out_specs=[tok(w) for w in widths],
        compiler_params=_params(2),
        name="inproj",
    )(*args)
    return dict(zip(names, outs))


def _score_pass(q, kv_blocks, s_ref, m_ref, m_extra=None):
    n_rows = q.shape[0]
    col = 0
    for idx, (load_k, _, ok, n) in enumerate(kv_blocks):
        s = _mm_t(q, load_k())
        if ok is not None:
            s = jnp.where(ok, s, NEG_INF)
        s_ref[:, col:col + n] = s
        block_max = functools.reduce(jnp.maximum, [s[:, c * LANES:(c + 1) * LANES] for c in range(n // LANES)])
        m_ref[...] = block_max if idx == 0 else jnp.maximum(m_ref[...], block_max)
        col += n
        yield
    m = jnp.max(m_ref[...], axis=-1, keepdims=True)
    if m_extra is not None:
        m = jnp.maximum(m, m_extra)
    m_ref[...] = jnp.broadcast_to(m, (n_rows, LANES))
    return m


def _value_pass(kv_blocks, s_ref, m_ref, l_ref=None):
    col, o = 0, None
    for idx, (_, load_v, _, n) in enumerate(kv_blocks):
        ps = [jnp.exp2(s_ref[:, col + c * LANES:col + (c + 1) * LANES] - m_ref[...]) for c in range(n // LANES)]
        if l_ref is not None:
            block_sum = functools.reduce(lambda a, b: a + b, ps)
            l_ref[...] = block_sum if idx == 0 else l_ref[...] + block_sum
        pv = _mm(jnp.concatenate(ps, axis=-1), load_v())
        o = pv if o is None else o + pv
        col += n
        yield
    l = jnp.sum(l_ref[...], axis=-1, keepdims=True) if l_ref is not None else None
    return o, l


def _step_together(gens):
    results, live = [None] * len(gens), list(range(len(gens)))
    while live:
        for i in list(live):
            try:
                next(gens[i])
            except StopIteration as stop:
                results[i] = stop.value
                live.remove(i)
    return results


def _run_units(units):
    n_stage = len(units[0])
    done = {}
    for t in range(len(units) + n_stage - 1):
        owners = [(t - k, k) for k in range(n_stage) if 0 <= t - k < len(units)]
        gens = [units[i][k]() if k == 0 else units[i][k](done.pop((i, k - 1))) for i, k in owners]
        for owner, result in zip(owners, _step_together(gens)):
            done[owner] = result


def _key_blocks(kc_ref, vc_ref, kl_ref, vl_ref, k_sl, v_sl):
    blocks = [(lambda: kc_ref[:, k_sl], lambda: vc_ref[:, v_sl], None, kc_ref.shape[0])]
    if kl_ref is not None:
        n_lat = kl_ref.shape[0]
        kb = min(KEY_BLOCK, n_lat)
        for j in range(n_lat // kb):
            rows = slice(j * kb, (j + 1) * kb)
            blocks.append((lambda rows=rows: kl_ref[rows, k_sl], lambda rows=rows: vl_ref[rows, v_sl], None, kb))
    return blocks


def _unit_mask(unit):
    lane = lax.broadcasted_iota(jnp.int32, (1, LANES), 1)
    return jnp.where((lane // 32) % 2 == unit, 1.0, 0.0).astype(MXU_DTYPE)


def _attn_a_kernel(*refs, lam_init, has_lat):
    if has_lat:
        lam_ref, gd_ref, q_ref, kc_ref, vc_ref, kl_ref, vl_ref, o_ref, s_ref, m_ref, l_ref = refs
    else:
        lam_ref, gd_ref, q_ref, kc_ref, vc_ref, o_ref, s_ref, m_ref, l_ref = refs
        kl_ref = vl_ref = None
    tq = q_ref.shape[0]
    lp = lam_ref[...]
    lam = (jnp.exp(jnp.sum(lp[0:1] * lp[1:2], axis=-1, keepdims=True))
           - jnp.exp(jnp.sum(lp[2:3] * lp[3:4], axis=-1, keepdims=True)) + lam_init)
    m0, m1 = _unit_mask(0), _unit_mask(1)

    def unit(h):
        sl = slice(h * LANES, (h + 1) * LANES)
        slot = h % N_SLOTS
        blocks = _key_blocks(kc_ref, vc_ref, kl_ref, vl_ref, sl, sl)
        s_h, m_h, l_h = s_ref.at[slot], m_ref.at[slot], l_ref.at[slot]

        def score():
            q = q_ref[:, sl]
            qq = jnp.concatenate([q * m0, q * m1], axis=0)
            yield from _score_pass(qq, blocks, s_h, m_h)

        def finish(_):
            o2, l2 = yield from _value_pass(blocks, s_h, m_h, l_h)
            r = 1.0 / l2
            o = o2[:tq] * r[:tq] - o2[tq:] * (lam * r[tq:])
            o = _rms(o, gd_ref[...]) * (1.0 - lam_init)
            o_ref[:, sl] = o.astype(o_ref.dtype)

        return score, finish

    _run_units([unit(h) for h in range(HA)])


def _attn_b_kernel(*refs, has_lat):
    if has_lat:
        q_ref, kc_ref, vc_ref, kl_ref, vl_ref, o_ref, s_ref, m_ref = refs
    else:
        q_ref, kc_ref, vc_ref, o_ref, s_ref, m_ref = refs
        kl_ref = vl_ref = None
    lane = lax.broadcasted_iota(jnp.int32, (1, LANES), 1)
    o_even = {}

    def unit(h):
        sl = slice(h * LANES, (h + 1) * LANES)
        slot = h % N_SLOTS
        blocks = _key_blocks(kc_ref, vc_ref, kl_ref, vl_ref, sl, sl)
        s_h, m_h = s_ref.at[slot], m_ref.at[slot]

        def finish(_):
            o, _ = yield from _value_pass(blocks, s_h, m_h)
            one = _ones_lane_of(h)
            o = o * (1.0 / o[:, one:one + 1])
            if h % 2 == 0:
                o_even[h // 2] = o
            else:
                pair = h // 2
                o_ref[:, pair * LANES:(pair + 1) * LANES] = jnp.where(lane < HALF, o_even.pop(pair), o).astype(o_ref.dtype)

        return (lambda: _score_pass(q_ref[:, sl], blocks, s_h, m_h)), finish

    _run_units([unit(h) for h in range(HB)])


def _attn_c_kernel(*refs, has_lat, n_lat):
    if has_lat:
        sink_ref, q_ref, kc_ref, vc_ref, kl_ref, vl_ref, o_ref, s_ref, m_ref = refs
    else:
        sink_ref, q_ref, kc_ref, vc_ref, o_ref, s_ref, m_ref = refs
    tq = q_ref.shape[0]
    if has_lat:
        band = tq + 2 * WINDOW
        q0 = pl.program_id(1) * tq
        start = pl.multiple_of(jnp.minimum(jnp.maximum(q0 - WINDOW, 0), n_lat - band), WINDOW)
        qpos = q0 + lax.broadcasted_iota(jnp.int32, (tq, 1), 0)
        kpos = start + lax.broadcasted_iota(jnp.int32, (1, band), 1)
        ok = jnp.abs(qpos - kpos) <= WINDOW
    lane = lax.broadcasted_iota(jnp.int32, (1, LANES), 1)
    masks = (_unit_mask(0), _unit_mask(1))
    o_kv0 = {}

    def unit(g, kv):
        sl = slice(g * LANES, (g + 1) * LANES)
        v_sl = slice(kv * LANES, (kv + 1) * LANES)
        blocks = [(lambda: kc_ref[...], lambda: vc_ref[:, v_sl], None, kc_ref.shape[0])]
        if has_lat:
            blocks.append((lambda: kl_ref[pl.ds(start, band), :], lambda: vl_ref[pl.ds(start, band), v_sl], ok, band))
        slot = (g * KVC + kv) % N_SLOTS
        s_u, m_u = s_ref.at[slot], m_ref.at[slot]

        def score():
            sink = jnp.full((1, 1), sink_ref[kv * G_C + g] * LOG2E, F32)
            m = yield from _score_pass(q_ref[:, sl] * masks[kv], blocks, s_u, m_u, m_extra=sink)
            return jnp.exp2(sink - m)

        def finish(sink_term):
            o, _ = yield from _value_pass(blocks, s_u, m_u)
            one = _ones_lane_of(kv)
            o = o * (1.0 / (o[:, one:one + 1] + sink_term))
            if kv == 0:
                o_kv0[g] = o
            else:
                o_ref[:, sl] = jnp.where(lane < HALF, o_kv0.pop(g), o).astype(o_ref.dtype)

        return score, finish

    _run_units([unit(g, kv) for g in range(G_C) for kv in range(KVC)])


def _attention(kind, q, kv_ctx, kv_lat, tq, extra_args, extra_specs, **static):
    nb, nq, qw = q.shape
    has_lat = kv_lat is not None
    full = lambda a: pl.BlockSpec((None,) + a.shape[1:], lambda b, i: (b, 0, 0))
    in_specs = list(extra_specs) + [pl.BlockSpec((None, tq, qw), lambda b, i: (b, i, 0))]
    in_specs += [full(kv_ctx[0]), full(kv_ctx[1])]
    args = list(extra_args) + [q, kv_ctx[0], kv_ctx[1]]
    n_keys = kv_ctx[0].shape[1]
    if has_lat:
        in_specs += [full(kv_lat[0]), full(kv_lat[1])]
        args += [kv_lat[0], kv_lat[1]]
        n_keys += (tq + 2 * WINDOW) if kind == "c" else kv_lat[0].shape[1]
    n_rows = 2 * tq if kind == "a" else tq
    scratch = [pltpu.VMEM((N_SLOTS, n_rows, n_keys), F32), pltpu.VMEM((N_SLOTS, n_rows, LANES), F32)]
    if kind == "a":
        scratch.append(pltpu.VMEM((N_SLOTS, n_rows, LANES), F32))
    body = {"a": _attn_a_kernel, "b": _attn_b_kernel, "c": _attn_c_kernel}[kind]
    return pl.pallas_call(
        functools.partial(body, has_lat=has_lat, **static),
        out_shape=jax.ShapeDtypeStruct((nb, nq, BRANCH_W), MXU_DTYPE),
        grid=(nb, nq // tq),
        in_specs=in_specs,
        out_specs=pl.BlockSpec((None, tq, BRANCH_W), lambda b, i: (b, i, 0)),
        scratch_shapes=scratch,
        compiler_params=_params(2),
        name="attn_" + kind,
    )(*args)


def _mixers(qs, ctx_kv, lat_kv, wts, lam_init, tq):
    has_lat = lat_kv is not None
    pick = lambda d, k, v: (d[k], d[v])
    ya = _attention("a", qs["qa"], pick(ctx_kv, "ka", "va"), pick(lat_kv, "ka", "va") if has_lat else None, tq,
                    [wts["lam"], wts["g_diff"]], [_resident((4, DA)), _resident((1, 2 * DA))], lam_init=lam_init)
    n_q = qs["qb"].shape[1]
    tq_b = 2 * tq if n_q % (2 * tq) == 0 else tq
    yb = _attention("b", qs["qb"], pick(ctx_kv, "kb", "vb"), pick(lat_kv, "kb", "vb") if has_lat else None, tq_b,
                    [], [])
    n_lat = lat_kv["kc"].shape[1] if has_lat else 0
    yc = _attention("c", qs["qc"], pick(ctx_kv, "kc", "vc"), pick(lat_kv, "kc", "vc") if has_lat else None, tq,
                    [wts["sink"]], [pl.BlockSpec(memory_space=pltpu.SMEM)], n_lat=n_lat)
    return ya, yb, yc


def _merge_kernel(h_ref, mod_ref, g1_ref, ya_ref, yb_ref, yc_ref, wg_ref, bg_ref, wb_ref, wo_ref, o_ref):
    x = h_ref[...]
    d = x.shape[-1]
    u = _norm_mod(x, g1_ref[...], mod_ref[0:1, :], mod_ref[1:2, :]).astype(MXU_DTYPE)
    acc = None
    for n, y_ref in enumerate((ya_ref, yb_ref, yc_ref)):
        gate = _sigmoid(jnp.dot(u, wg_ref[:, n * d:(n + 1) * d], preferred_element_type=F32)
                        + bg_ref[:, n * d:(n + 1) * d])
        term = gate * jnp.dot(y_ref[...], wb_ref[n], preferred_element_type=F32)
        acc = term if acc is None else acc + term
    o_ref[...] = x + mod_ref[2:3, :] * _mm(acc, wo_ref[...])


def _merge(h, mod_l, mod_row0, wts, ys, tm):
    nb, ns, d = h.shape
    tok = lambda w: pl.BlockSpec((None, tm, w), lambda b, i: (b, i, 0))
    return pl.pallas_call(
        _merge_kernel,
        out_shape=jax.ShapeDtypeStruct(h.shape, F32),
        grid=(nb, ns // tm),
        in_specs=[
            tok(d),
            pl.BlockSpec((None, N_MOD, d), (lambda b, i: (b + 1, 0, 0)) if mod_row0 else (lambda b, i: (0, 0, 0))),
            _resident((1, d)), tok(BRANCH_W), tok(BRANCH_W), tok(BRANCH_W),
            _resident((d, N_BRANCH * d)), _resident((1, N_BRANCH * d)),
            _resident((N_BRANCH, BRANCH_W, d)), _resident((d, d)),
        ],
        out_specs=tok(d),
        compiler_params=_params(2),
        name="merge",
    )(h, mod_l, wts["g1"], *ys, wts["w_gate"], wts["b_gate"], wts["w_branch"], wts["w_o"])


FF_CHUNK = 1024


def _ffn_kernel(*refs, final_norm):
    if final_norm:
        h_ref, hp_ref, hn_ref, mod_ref, g2_ref, wu_ref, cw_ref, cb_ref, wd_ref, gf_ref, o_ref = refs
    else:
        h_ref, hp_ref, hn_ref, mod_ref, g2_ref, wu_ref, cw_ref, cb_ref, wd_ref, o_ref = refs
    i, n_tiles = pl.program_id(1), pl.num_programs(1)
    x = h_ref[...]
    tm, d = x.shape
    d_ff = wd_ref.shape[0]
    shift, scale, g2 = mod_ref[3:4, :], mod_ref[4:5, :], g2_ref[...]
    u_prev = jnp.where(i > 0, _norm_mod(hp_ref[...], g2, shift, scale), 0.0)
    u_next = jnp.where(i < n_tiles - 1, _norm_mod(hn_ref[...], g2, shift, scale), 0.0)
    u = jnp.concatenate([u_prev, _norm_mod(x, g2, shift, scale), u_next], axis=0).astype(MXU_DTYPE)
    n_ext = tm + 2 * SUBLANES

    def conv(a, lo, width):
        w = cw_ref[:, lo:lo + width]
        prev = pltpu.roll(a, 1, axis=0)[SUBLANES:SUBLANES + tm]
        nxt = pltpu.roll(a, n_ext - 1, axis=0)[SUBLANES:SUBLANES + tm]
        return (w[0:1] * prev + w[1:2] * a[SUBLANES:SUBLANES + tm] + w[2:3] * nxt) + cb_ref[:, lo:lo + width]

    def up(lo):
        width = min(FF_CHUNK, d_ff - lo)
        return (jnp.dot(u, wu_ref[:, lo:lo + width], preferred_element_type=F32),
                jnp.dot(u, wu_ref[:, d_ff + lo:d_ff + lo + width], preferred_element_type=F32))

    chunks = list(range(0, d_ff, FF_CHUNK))
    acc, ahead = None, up(chunks[0])
    for c, lo in enumerate(chunks):
        width = min(FF_CHUNK, d_ff - lo)
        a_gate, a_val = ahead
        if c + 1 < len(chunks):
            ahead = up(chunks[c + 1])
        gate = conv(a_gate, lo, width)
        val = conv(a_val, d_ff + lo, width)
        z = gate * _sigmoid(gate) * val
        part = _mm(z, wd_ref[lo:lo + width, :])
        acc = part if acc is None else acc + part
    y = x + mod_ref[5:6, :] * acc
    if final_norm:
        y = _rms(y, gf_ref[...])
    o_ref[...] = y


def _ffn(h, mod_l, mod_row0, wts, tm, g_final=None):
    nb, ns, d = h.shape
    d_ff = wts["w_down"].shape[0]
    per = tm // SUBLANES
    last = ns // SUBLANES - 1
    tok = pl.BlockSpec((None, tm, d), lambda b, i: (b, i, 0))
    in_specs = [
        tok,
        pl.BlockSpec((None, SUBLANES, d), lambda b, i: (b, jnp.maximum(i * per - 1, 0), 0)),
        pl.BlockSpec((None, SUBLANES, d), lambda b, i: (b, jnp.minimum((i + 1) * per, last), 0)),
        pl.BlockSpec((None, N_MOD, d), (lambda b, i: (b + 1, 0, 0)) if mod_row0 else (lambda b, i: (0, 0, 0))),
        _resident((1, d)), _resident((d, 2 * d_ff)), _resident((CONV_W, 2 * d_ff)), _resident((1, 2 * d_ff)),
        _resident((d_ff, d)),
    ]
    args = [h, h, h, mod_l, wts["g2"], wts["w_up"], wts["conv_w"], wts["conv_b"], wts["w_down"]]
    if g_final is not None:
        in_specs.append(_resident((1, d)))
        args.append(g_final)
    return pl.pallas_call(
        functools.partial(_ffn_kernel, final_norm=g_final is not None),
        out_shape=jax.ShapeDtypeStruct(h.shape, F32),
        grid=(nb, ns // tm),
        in_specs=in_specs,
        out_specs=tok,
        compiler_params=_params(2),
        name="conv_ffn",
    )(*args)


def _layer_weights(l, g_norm1, w_in, lam, g_diff, g_cq, w_uq, g_ckv, w_ukv, sink, w_branch, w_gate, b_gate,
                   w_o, g_norm2, w_up, conv_w, conv_b, w_down):
    mx = lambda a: a.astype(MXU_DTYPE)
    row = lambda a: a.reshape(1, -1).astype(F32)
    wb = w_branch[l]
    row_src = _wbranch_c_src()
    wb_c = jnp.concatenate([wb[2, int(r):int(r) + DC] for r in row_src[::DC]], axis=0)
    wb = jnp.stack([wb[0], wb[1], wb_c])
    return {
        "g1": row(g_norm1[l]), "w_in": mx(_take_cols(w_in[l], _in_proj_src())),
        "lam": lam[l].astype(F32), "g_diff": row(g_diff[l]),
        "g_cq": row(g_cq[l]), "w_uq": mx(_take_cols(w_uq[l], _wuq_src())),
        "g_ckv": row(g_ckv[l]), "w_kv": mx(_take_cols(w_ukv[l], _wkv_src())),
        "sink": sink[l].astype(F32),
        "w_gate": mx(w_gate[l]), "b_gate": row(b_gate[l]), "w_branch": mx(wb), "w_o": mx(w_o[l]),
        "g2": row(g_norm2[l]), "w_up": mx(w_up[l]), "conv_w": conv_w[l].astype(F32), "conv_b": row(conv_b[l]),
        "w_down": mx(w_down[l]),
    }


def _tile(n, want):
    t = min(n, want)
    assert n % t == 0, (n, t)
    return t


def kernel(x, c, ctx, c_ctx, w_ada, b_ada, g_norm1, w_in, lam, g_diff, g_cq, w_uq, g_ckv, w_ukv, sink, w_branch,
           w_gate, b_gate, w_o, g_norm2, w_up, conv_w, conv_b, w_down, g_final):
    n_batch, n_lat, d = x.shape
    n_ctx = ctx.shape[1]
    n_layers = w_ada.shape[0]
    assert n_lat % GRID_W == 0 and n_lat % WINDOW == 0
    tm_lat, tm_ctx = _tile(n_lat, 512), _tile(n_ctx, 256)
    tq_lat, tq_ctx = _tile(n_lat, 256), _tile(n_ctx, 256)
    assert n_lat >= tq_lat + 2 * WINDOW

    n_rows = -(-(1 + n_batch) // SUBLANES) * SUBLANES
    c_all = jnp.concatenate([c_ctx[None, :], c, jnp.zeros((n_rows - 1 - n_batch, d), c.dtype)], axis=0)
    mod = _modulation(c_all.astype(F32), w_ada, b_ada).reshape(n_layers, n_rows, N_MOD, d)
    tables = _rope_tables(n_lat)

    h, hc = x.astype(F32), ctx.astype(F32)
    for l in range(n_layers):
        ctx_out = l < n_layers - 1
        lam_init = 0.8 - 0.6 * math.exp(-0.3 * l)
        wts = _layer_weights(l, g_norm1, w_in, lam, g_diff, g_cq, w_uq, g_ckv, w_ukv, sink, w_branch, w_gate,
                             b_gate, w_o, g_norm2, w_up, conv_w, conv_b, w_down)
        lat = _inproj(h, mod[l], True, wts, tables, tm_lat, True)
        cx = _inproj(hc, mod[l], False, wts, None, tm_ctx, ctx_out)
        ys = _mixers(lat, cx, lat, wts, lam_init, tq_lat)
        h = _merge(h, mod[l], True, wts, ys, tm_lat)
        h = _ffn(h, mod[l], True, wts, tm_lat, g_final.reshape(1, d).astype(F32) if not ctx_out else None)
        if ctx_out:
            ys_c = _mixers(cx, cx, None, wts, lam_init, tq_ctx)
            hc = _merge(hc, mod[l], False, wts, ys_c, tm_ctx)
            hc = _ffn(hc, mod[l], False, wts, tm_ctx)
    return h
```

```python
import functools
import math

import numpy as np
import jax
import jax.numpy as jnp
from jax import lax
from jax.experimental import pallas as pl
from jax.experimental.pallas import tpu as pltpu

F32 = jnp.float32
MXU_DTYPE = jnp.bfloat16

GRID_W = 64
HA, DA = 4, 64
HB, NOPE_B, ROPE_B, V_B = 8, 64, 32, 64
Q_RANK, KV_RANK = 256, 128
HC, KVC, DC = 8, 2, 64
G_C = HC // KVC
WINDOW = 128
BRANCH_W = 512
N_BRANCH = 3
CONV_W = 3
ROPE_BASE = 10000.0
EPS = 1e-6
NEG_INF = -1e30
N_MOD = 6
MLA_SCALE = (NOPE_B + ROPE_B) ** -0.5
IN_WIDTHS = (HA * 2 * DA, HA * 2 * DA, HA * 2 * DA, Q_RANK, KV_RANK, ROPE_B, HC * DC, KVC * DC, KVC * DC)
IN_OFFS = tuple(int(v) for v in np.cumsum((0,) + IN_WIDTHS))

LANES = 128
SUBLANES = 8
HALF = LANES // 2
VMEM_LIMIT = 56 * 1024 * 1024

SEG_QA, SEG_KA, SEG_VA, SEG_QC = 0, 512, 1024, 1536
SEG_KC, SEG_VC, SEG_CQ, SEG_CKV, SEG_KR = 2048, 2176, 2432, 2688, 2816
NP_IN = 2944
LOG2E = 1.4426950408889634
KEY_BLOCK = 512
N_SLOTS = 2

def _slab64_src(base, unit_of_group):
    src = np.zeros(LANES, np.int64)
    for grp in range(4):
        unit, half = grp % 2, grp // 2
        for r in range(32):
            src[grp * 32 + r] = base + unit_of_group[unit] + half * 32 + r
    return src


def _in_proj_src():
    src = -np.ones(NP_IN, np.int64)
    qa0, ka0, va0, cq0, ckv0, kr0, qc0, kc0, vc0 = IN_OFFS[:9]
    for h in range(HA):
        src[SEG_QA + h * LANES:SEG_QA + (h + 1) * LANES] = _slab64_src(qa0 + h * 2 * DA, (0, DA))
        src[SEG_KA + h * LANES:SEG_KA + (h + 1) * LANES] = _slab64_src(ka0 + h * 2 * DA, (0, DA))
    src[SEG_VA:SEG_VA + 512] = va0 + np.arange(512)
    for g in range(G_C):
        src[SEG_QC + g * LANES:SEG_QC + (g + 1) * LANES] = _slab64_src(qc0, (g * DC, (G_C + g) * DC))
    src[SEG_KC:SEG_KC + LANES] = _slab64_src(kc0, (0, DC))
    for kv in range(KVC):
        v_lo = SEG_VC + kv * LANES + (kv % 2) * HALF
        src[v_lo:v_lo + DC] = vc0 + kv * DC + np.arange(DC)
    src[SEG_CQ:SEG_CQ + Q_RANK] = cq0 + np.arange(Q_RANK)
    src[SEG_CKV:SEG_CKV + KV_RANK] = ckv0 + np.arange(KV_RANK)
    src[SEG_KR:SEG_KR + 16] = kr0 + np.arange(16)
    src[SEG_KR + HALF:SEG_KR + HALF + 16] = kr0 + 16 + np.arange(16)
    return src


def _mla_slab_src(nope0, rope0):
    src = -np.ones(LANES, np.int64)
    if rope0 is not None:
        src[0:16] = rope0 + np.arange(16)
        src[HALF:HALF + 16] = rope0 + 16 + np.arange(16)
    src[16:64] = nope0 + np.arange(48)
    src[80:96] = nope0 + 48 + np.arange(16)
    return src


def _wuq_src():
    src = -np.ones(HB * LANES, np.int64)
    for h in range(HB):
        b0 = h * (NOPE_B + ROPE_B)
        src[h * LANES:(h + 1) * LANES] = _mla_slab_src(b0, b0 + NOPE_B)
    return src


def _wkv_src():
    src = -np.ones(2 * HB * LANES, np.int64)
    for h in range(HB):
        b0 = h * (NOPE_B + V_B)
        src[h * LANES:(h + 1) * LANES] = _mla_slab_src(b0, None)
        v_lo = HB * LANES + h * LANES + (h % 2) * HALF
        src[v_lo:v_lo + V_B] = b0 + NOPE_B + np.arange(V_B)
    return src


def _wbranch_c_src():
    src = np.zeros(BRANCH_W, np.int64)
    for g in range(G_C):
        for kv in range(KVC):
            src[g * LANES + kv * DC:g * LANES + (kv + 1) * DC] = (kv * G_C + g) * DC + np.arange(DC)
    return src


def _take_cols(w, src):
    parts, start = [], 0
    for i in range(1, len(src) + 1):
        if i == len(src) or (src[i] != src[i - 1] + 1 if src[i - 1] >= 0 else src[i] >= 0):
            if src[start] < 0:
                parts.append(jnp.zeros(w.shape[:-1] + (i - start,), w.dtype))
            else:
                parts.append(w[..., int(src[start]):int(src[start]) + i - start])
            start = i
    return jnp.concatenate(parts, axis=-1)


def _rope_tables(n_tok):
    def table(rot_dim):
        n_freq = rot_dim // 4
        inv = ROPE_BASE ** (-jnp.arange(n_freq, dtype=F32) / n_freq)
        n_rows = n_tok // GRID_W
        rows = jnp.repeat(jnp.arange(n_rows, dtype=F32), GRID_W)
        cols = jnp.tile(jnp.arange(GRID_W, dtype=F32), n_rows)
        ang = jnp.concatenate([rows[:, None] * inv, cols[:, None] * inv], axis=-1)
        return jnp.cos(ang), jnp.sin(ang)

    cos64, sin64 = table(DA)
    cosb, sinb = table(ROPE_B)
    cos64_t = jnp.concatenate([cos64] * 4, axis=-1)
    sin64_t = jnp.concatenate([-sin64, -sin64, sin64, sin64], axis=-1)
    ones = jnp.ones((n_tok, HALF - 16), F32)
    zeros = jnp.zeros((n_tok, HALF - 16), F32)
    cosb_t = jnp.concatenate([cosb, ones, cosb, ones], axis=-1)
    sinb_t = jnp.concatenate([-sinb, zeros, sinb, zeros], axis=-1)
    return cos64_t, sin64_t, cosb_t, sinb_t


def _mm(a, b):
    return jnp.dot(a.astype(MXU_DTYPE), b.astype(MXU_DTYPE), preferred_element_type=F32)


def _mm_t(a, b):
    return lax.dot_general(a.astype(MXU_DTYPE), b.astype(MXU_DTYPE), (((1,), (1,)), ((), ())),
                           preferred_element_type=F32)


def _rms(x, g):
    return x * lax.rsqrt(jnp.mean(x * x, axis=-1, keepdims=True) + EPS) * g


def _norm_mod(x, g, shift, scale):
    return _rms(x, g) * (1.0 + scale) + shift


def _sigmoid(x):
    return 1.0 / (1.0 + jnp.exp(-x))


def _rope_slab(x, cos_t, sin_t):
    return x * cos_t + pltpu.roll(x, HALF, axis=1) * sin_t


def _ones_lanes(n_slab):
    lane = lax.broadcasted_iota(jnp.int32, (1, n_slab * LANES), 1)
    slab_parity = (lane // LANES) % 2
    upper_half = (lane % LANES) // HALF
    return jnp.where(slab_parity + upper_half == 1, 1.0, 0.0)


def _normalise_half(o, extra=None):
    den = o if extra is None else o + extra
    return o * pltpu.roll(1.0 / den, HALF, axis=1)


def _resident(shape):
    nd = len(shape)
    return pl.BlockSpec(shape, lambda *_: (0,) * nd, pipeline_mode=pl.Buffered(1))


def _params(n_grid):
    return pltpu.CompilerParams(dimension_semantics=("arbitrary",) * n_grid, vmem_limit_bytes=VMEM_LIMIT)


def _mod_kernel(c_ref, w_ref, b_ref, o_ref):
    c = c_ref[...]
    s = c * _sigmoid(c)
    o_ref[...] = _mm(s, w_ref[...]) + b_ref[...]


def _modulation(c_all, w_ada, b_ada):
    n_layers, d, n_out = w_ada.shape
    rows = c_all.shape[0]
    tn = 1536
    return pl.pallas_call(
        _mod_kernel,
        out_shape=jax.ShapeDtypeStruct((n_layers, rows, n_out), F32),
        grid=(n_layers, n_out // tn),
        in_specs=[
            pl.BlockSpec((rows, d), lambda l, j: (0, 0)),
            pl.BlockSpec((None, d, tn), lambda l, j: (l, 0, j)),
            pl.BlockSpec((None, 1, tn), lambda l, j: (l, 0, j)),
        ],
        out_specs=pl.BlockSpec((None, rows, tn), lambda l, j: (l, 0, j)),
        compiler_params=_params(2),
        name="modulation",
    )(c_all, w_ada, b_ada.reshape(n_layers, 1, n_out))


def _inproj_kernel(*refs, rope, need_q):
    it = iter(refs)
    h_ref, mod_ref, g1_ref, w_ref, gcq_ref, wuq_ref, gckv_ref, wkv_ref = (next(it) for _ in range(8))
    if rope:
        cos64_ref, sin64_ref, cosb_ref, sinb_ref = (next(it) for _ in range(4))
    if need_q:
        qa_ref, qb_ref, qc_ref = (next(it) for _ in range(3))
    ka_ref, va_ref, kb_ref, vb_ref, kc_ref, vc_ref = (next(it) for _ in range(6))

    u = _norm_mod(h_ref[...], g1_ref[...], mod_ref[0:1, :], mod_ref[1:2, :]).astype(MXU_DTYPE)

    def proj(lo, width):
        return jnp.dot(u, w_ref[:, lo:lo + width], preferred_element_type=F32)

    def rot64(x):
        return _rope_slab(x, cos64_ref[...], sin64_ref[...]) if rope else x

    def rotb(x):
        return _rope_slab(x, cosb_ref[...], sinb_ref[...]) if rope else x

    def store_rot64(dst_ref, lo, n_slab, scale):
        p = proj(lo, n_slab * LANES)
        for s in range(n_slab):
            y = rot64(p[:, s * LANES:(s + 1) * LANES])
            if scale != 1.0:
                y = y * scale
            dst_ref[:, s * LANES:(s + 1) * LANES] = y.astype(dst_ref.dtype)

    low = proj(SEG_CQ, Q_RANK + KV_RANK + LANES)
    if need_q:
        store_rot64(qa_ref, SEG_QA, HA, DA ** -0.5 * LOG2E)
        store_rot64(qc_ref, SEG_QC, G_C, DC ** -0.5 * LOG2E)
    store_rot64(ka_ref, SEG_KA, HA, 1.0)
    va_ref[...] = proj(SEG_VA, 512).astype(va_ref.dtype)
    kcvc = proj(SEG_KC, (1 + KVC) * LANES)
    kc_ref[...] = rot64(kcvc[:, :LANES]).astype(kc_ref.dtype)
    vc_ref[...] = (kcvc[:, LANES:] + _ones_lanes(KVC)).astype(vc_ref.dtype)

    if need_q:
        cq = _rms(low[:, :Q_RANK], gcq_ref[...])
        qh = _mm(cq, wuq_ref[...])
        for h in range(HB):
            y = rotb(qh[:, h * LANES:(h + 1) * LANES]) * (MLA_SCALE * LOG2E)
            qb_ref[:, h * LANES:(h + 1) * LANES] = y.astype(qb_ref.dtype)
    ckv = _rms(low[:, Q_RANK:Q_RANK + KV_RANK], gckv_ref[...])
    kvh = _mm(ckv, wkv_ref[...])
    kr = rotb(low[:, Q_RANK + KV_RANK:])
    for h in range(HB):
        kb_ref[:, h * LANES:(h + 1) * LANES] = (kvh[:, h * LANES:(h + 1) * LANES] + kr).astype(kb_ref.dtype)
    vb_ref[...] = (kvh[:, HB * LANES:] + _ones_lanes(HB)).astype(vb_ref.dtype)


def _inproj(h, mod_l, mod_row0, wts, tables, tm, need_q):
    nb, ns, d = h.shape
    rope = tables is not None
    tok = lambda w: pl.BlockSpec((None, tm, w), lambda b, i: (b, i, 0))
    in_specs = [
        tok(d),
        pl.BlockSpec((None, N_MOD, d), (lambda b, i: (b + 1, 0, 0)) if mod_row0 else (lambda b, i: (0, 0, 0))),
        _resident((1, d)), _resident((d, NP_IN)),
        _resident((1, Q_RANK)), _resident((Q_RANK, HB * LANES)),
        _resident((1, KV_RANK)), _resident((KV_RANK, 2 * HB * LANES)),
    ]
    args = [h, mod_l, wts["g1"], wts["w_in"], wts["g_cq"], wts["w_uq"], wts["g_ckv"], wts["w_kv"]]
    if rope:
        in_specs += [pl.BlockSpec((tm, LANES), lambda b, i: (i, 0))] * 4
        args += list(tables)
    names, widths = [], []
    if need_q:
        names += ["qa", "qb", "qc"]
        widths += [512, HB * LANES, 512]
    names += ["ka", "va", "kb", "vb", "kc", "vc"]
    widths += [512, 512, HB * LANES, HB * LANES, LANES, KVC * LANES]
    outs = pl.pallas_call(
        functools.partial(_inproj_kernel, rope=rope, need_q=need_q),
        out_shape=[jax.ShapeDtypeStruct((nb, ns, w), MXU_DTYPE) for w in widths],
        grid=(nb, ns // tm),
        in_specs=in_specs,
        out_specs=[tok(w) for w in widths],
        compiler_params=_params(2),
        name="inproj",
    )(*args)
    return dict(zip(names, outs))


def _score_pass(q, kv_blocks, s_ref, m_ref, m_extra=None):
    n_rows = q.shape[0]
    col = 0
    for idx, (load_k, _, ok, n) in enumerate(kv_blocks):
        s = _mm_t(q, load_k())
        if ok is not None:
            s = jnp.where(ok, s, NEG_INF)
        s_ref[:, col:col + n] = s
        block_max = functools.reduce(jnp.maximum, [s[:, c * LANES:(c + 1) * LANES] for c in range(n // LANES)])
        m_ref[...] = block_max if idx == 0 else jnp.maximum(m_ref[...], block_max)
        col += n
        yield
    m = jnp.max(m_ref[...], axis=-1, keepdims=True)
    if m_extra is not None:
        m = jnp.maximum(m, m_extra)
    m_ref[...] = jnp.broadcast_to(m, (n_rows, LANES))
    return m


def _value_pass(kv_blocks, s_ref, m_ref, l_ref=None):
    col, o = 0, None
    for idx, (_, load_v, _, n) in enumerate(kv_blocks):
        ps = [jnp.exp2(s_ref[:, col + c * LANES:col + (c + 1) * LANES] - m_ref[...]) for c in range(n // LANES)]
        if l_ref is not None:
            block_sum = functools.reduce(lambda a, b: a + b, ps)
            l_ref[...] = block_sum if idx == 0 else l_ref[...] + block_sum
        pv = _mm(jnp.concatenate(ps, axis=-1), load_v())
        o = pv if o is None else o + pv
        col += n
        yield
    l = jnp.sum(l_ref[...], axis=-1, keepdims=True) if l_ref is not None else None
    return o, l


def _step_together(gens):
    results, live = [None] * len(gens), list(range(len(gens)))
    while live:
        for i in list(live):
            try:
                next(gens[i])
            except StopIteration as stop:
                results[i] = stop.value
                live.remove(i)
    return results


def _run_units(units):
    n_stage = len(units[0])
    done = {}
    for t in range(len(units) + n_stage - 1):
        owners = [(t - k, k) for k in range(n_stage) if 0 <= t - k < len(units)]
        gens = [units[i][k]() if k == 0 else units[i][k](done.pop((i, k - 1))) for i, k in owners]
        for owner, result in zip(owners, _step_together(gens)):
            done[owner] = result


def _key_blocks(kc_ref, vc_ref, kl_ref, vl_ref, k_sl, v_sl):
    blocks = [(lambda: kc_ref[:, k_sl], lambda: vc_ref[:, v_sl], None, kc_ref.shape[0])]
    if kl_ref is not None:
        n_lat = kl_ref.shape[0]
        kb = min(KEY_BLOCK, n_lat)
        for j in range(n_lat // kb):
            rows = slice(j * kb, (j + 1) * kb)
            blocks.append((lambda rows=rows: kl_ref[rows, k_sl], lambda rows=rows: vl_ref[rows, v_sl], None, kb))
    return blocks


def _unit_mask(unit):
    lane = lax.broadcasted_iota(jnp.int32, (1, LANES), 1)
    return jnp.where((lane // 32) % 2 == unit, 1.0, 0.0).astype(MXU_DTYPE)


def _units_a(lam_ref, gd_ref, q_ref, kc_ref, vc_ref, kl_ref, vl_ref, o_ref, s_ref, m_ref, lam_init):
    def ones_slab(n):
        return jnp.ones((n, LANES), MXU_DTYPE)
    tq = q_ref.shape[0]
    lp = lam_ref[...]
    lam = (jnp.exp(jnp.sum(lp[0:1] * lp[1:2], axis=-1, keepdims=True))
           - jnp.exp(jnp.sum(lp[2:3] * lp[3:4], axis=-1, keepdims=True)) + lam_init)
    m0, m1 = _unit_mask(0), _unit_mask(1)

    def unit(h):
        sl = slice(h * LANES, (h + 1) * LANES)
        slot = h % N_SLOTS
        blocks = [(load_k, lambda load_v=load_v, n=n: jnp.concatenate([load_v(), ones_slab(n)], axis=1), ok, n)
                  for load_k, load_v, ok, n in _key_blocks(kc_ref, vc_ref, kl_ref, vl_ref, sl, sl)]
        s_h, m_h = s_ref.at[slot], m_ref.at[slot]

        def score():
            q = q_ref[:, sl]
            qq = jnp.concatenate([q * m0, q * m1], axis=0)
            yield from _score_pass(qq, blocks, s_h, m_h)

        def finish(_):
            o_ext, _ = yield from _value_pass(blocks, s_h, m_h)
            o2 = o_ext[:, :LANES]
            r = 1.0 / o_ext[:, LANES:]
            o = o2[:tq] * r[:tq] - o2[tq:] * (lam * r[tq:])
            o = _rms(o, gd_ref[...]) * (1.0 - lam_init)
            o_ref[:, sl] = o.astype(o_ref.dtype)

        return score, finish

    return [unit(h) for h in range(HA)]


def _attn_b_kernel(*refs, has_lat):
    if has_lat:
        q_ref, kc_ref, vc_ref, kl_ref, vl_ref, o_ref, s_ref, m_ref = refs
    else:
        q_ref, kc_ref, vc_ref, o_ref, s_ref, m_ref = refs
        kl_ref = vl_ref = None
    lane = lax.broadcasted_iota(jnp.int32, (1, LANES), 1)
    o_even = {}

    def unit(h):
        sl = slice(h * LANES, (h + 1) * LANES)
        slot = h % N_SLOTS
        blocks = _key_blocks(kc_ref, vc_ref, kl_ref, vl_ref, sl, sl)
        s_h, m_h = s_ref.at[slot], m_ref.at[slot]

        def finish(_):
            o, _ = yield from _value_pass(blocks, s_h, m_h)
            o = _normalise_half(o)
            if h % 2 == 0:
                o_even[h // 2] = o
            else:
                pair = h // 2
                o_ref[:, pair * LANES:(pair + 1) * LANES] = jnp.where(lane < HALF, o_even.pop(pair), o).astype(o_ref.dtype)

        return (lambda: _score_pass(q_ref[:, sl], blocks, s_h, m_h)), finish

    _run_units([unit(h) for h in range(HB)])


def _units_c(sink_ref, q_ref, kc_ref, vc_ref, kl_ref, vl_ref, o_ref, s_ref, m_ref, n_lat):
    has_lat = kl_ref is not None
    tq = q_ref.shape[0]
    if has_lat:
        band = tq + 2 * WINDOW
        q0 = pl.program_id(1) * tq
        start = pl.multiple_of(jnp.minimum(jnp.maximum(q0 - WINDOW, 0), n_lat - band), WINDOW)
        qpos = q0 + lax.broadcasted_iota(jnp.int32, (tq, 1), 0)
        kpos = start + lax.broadcasted_iota(jnp.int32, (1, band), 1)
        ok = jnp.abs(qpos - kpos) <= WINDOW
    lane = lax.broadcasted_iota(jnp.int32, (1, LANES), 1)
    masks = (_unit_mask(0), _unit_mask(1))
    o_kv0 = {}

    def unit(g, kv):
        sl = slice(g * LANES, (g + 1) * LANES)
        v_sl = slice(kv * LANES, (kv + 1) * LANES)
        blocks = [(lambda: kc_ref[...], lambda: vc_ref[:, v_sl], None, kc_ref.shape[0])]
        if has_lat:
            blocks.append((lambda: kl_ref[pl.ds(start, band), :], lambda: vl_ref[pl.ds(start, band), v_sl], ok, band))
        slot = (g * KVC + kv) % N_SLOTS
        s_u, m_u = s_ref.at[slot], m_ref.at[slot]

        def score():
            sink = jnp.full((1, 1), sink_ref[kv * G_C + g] * LOG2E, F32)
            yield from _score_pass(q_ref[:, sl] * masks[kv], blocks, s_u, m_u, m_extra=sink)
            return sink

        def finish(sink):
            o, _ = yield from _value_pass(blocks, s_u, m_u)
            o = _normalise_half(o, extra=jnp.exp2(sink - m_u[...]))
            if kv == 0:
                o_kv0[g] = o
            else:
                o_ref[:, sl] = jnp.where(lane < HALF, o_kv0.pop(g), o).astype(o_ref.dtype)

        return score, finish

    return [unit(g, kv) for g in range(G_C) for kv in range(KVC)]


def _attn_a_kernel(*refs, lam_init, has_lat):
    refs = list(refs)
    if not has_lat:
        refs[5:5] = [None, None]
    _run_units(_units_a(*refs, lam_init))


def _attn_c_kernel(*refs, n_lat, has_lat):
    refs = list(refs)
    if not has_lat:
        refs[4:4] = [None, None]
    _run_units(_units_c(*refs, n_lat))


def _attention(kind, q, kv_ctx, kv_lat, tq, extra_args, extra_specs, **static):
    nb, nq, qw = q.shape
    has_lat = kv_lat is not None
    full = lambda a: pl.BlockSpec((None,) + a.shape[1:], lambda b, i: (b, 0, 0))
    in_specs = list(extra_specs) + [pl.BlockSpec((None, tq, qw), lambda b, i: (b, i, 0))]
    in_specs += [full(kv_ctx[0]), full(kv_ctx[1])]
    args = list(extra_args) + [q, kv_ctx[0], kv_ctx[1]]
    n_keys = kv_ctx[0].shape[1]
    if has_lat:
        in_specs += [full(kv_lat[0]), full(kv_lat[1])]
        args += [kv_lat[0], kv_lat[1]]
        n_keys += (tq + 2 * WINDOW) if kind == "c" else kv_lat[0].shape[1]
    n_rows = 2 * tq if kind == "a" else tq
    body = {"a": _attn_a_kernel, "b": _attn_b_kernel, "c": _attn_c_kernel}[kind]
    return pl.pallas_call(
        functools.partial(body, has_lat=has_lat, **static),
        out_shape=jax.ShapeDtypeStruct((nb, nq, BRANCH_W), MXU_DTYPE),
        grid=(nb, nq // tq),
        in_specs=in_specs,
        out_specs=pl.BlockSpec((None, tq, BRANCH_W), lambda b, i: (b, i, 0)),
        scratch_shapes=[pltpu.VMEM((N_SLOTS, n_rows, n_keys), F32), pltpu.VMEM((N_SLOTS, n_rows, LANES), F32)],
        compiler_params=_params(2),
        name="attn_" + kind,
    )(*args)


def _mixers(qs, ctx_kv, lat_kv, wts, lam_init, tq):
    pick = lambda d, k, v: (d[k], d[v]) if d is not None else None
    ya = _attention("a", qs["qa"], pick(ctx_kv, "ka", "va"), pick(lat_kv, "ka", "va"), tq,
                    [wts["lam"], wts["g_diff"]], [_resident((4, DA)), _resident((1, 2 * DA))], lam_init=lam_init)
    n_q = qs["qb"].shape[1]
    tq_b = 2 * tq if n_q % (2 * tq) == 0 else tq
    yb = _attention("b", qs["qb"], pick(ctx_kv, "kb", "vb"), pick(lat_kv, "kb", "vb"), tq_b, [], [])
    n_lat = lat_kv["kc"].shape[1] if lat_kv is not None else 0
    yc = _attention("c", qs["qc"], pick(ctx_kv, "kc", "vc"), pick(lat_kv, "kc", "vc"), tq,
                    [wts["sink"]], [pl.BlockSpec(memory_space=pltpu.SMEM)], n_lat=n_lat)
    return ya, yb, yc


def _merge_kernel(h_ref, mod_ref, g1_ref, ya_ref, yb_ref, yc_ref, wg_ref, bg_ref, wb_ref, wo_ref, o_ref):
    x = h_ref[...]
    d = x.shape[-1]
    u = _norm_mod(x, g1_ref[...], mod_ref[0:1, :], mod_ref[1:2, :]).astype(MXU_DTYPE)
    acc = None
    for n, y_ref in enumerate((ya_ref, yb_ref, yc_ref)):
        gate = _sigmoid(jnp.dot(u, wg_ref[:, n * d:(n + 1) * d], preferred_element_type=F32)
                        + bg_ref[:, n * d:(n + 1) * d])
        term = gate * jnp.dot(y_ref[...], wb_ref[n], preferred_element_type=F32)
        acc = term if acc is None else acc + term
    o_ref[...] = x + mod_ref[2:3, :] * _mm(acc, wo_ref[...])


def _merge(h, mod_l, mod_row0, wts, ys, tm):
    nb, ns, d = h.shape
    tok = lambda w: pl.BlockSpec((None, tm, w), lambda b, i: (b, i, 0))
    return pl.pallas_call(
        _merge_kernel,
        out_shape=jax.ShapeDtypeStruct(h.shape, F32),
        grid=(nb, ns // tm),
        in_specs=[
            tok(d),
            pl.BlockSpec((None, N_MOD, d), (lambda b, i: (b + 1, 0, 0)) if mod_row0 else (lambda b, i: (0, 0, 0))),
            _resident((1, d)), tok(BRANCH_W), tok(BRANCH_W), tok(BRANCH_W),
            _resident((d, N_BRANCH * d)), _resident((1, N_BRANCH * d)),
            _resident((N_BRANCH, BRANCH_W, d)), _resident((d, d)),
        ],
        out_specs=tok(d),
        compiler_params=_params(2),
        name="merge",
    )(h, mod_l, wts["g1"], *ys, wts["w_gate"], wts["b_gate"], wts["w_branch"], wts["w_o"])


FF_CHUNK = 512
FF_AHEAD = 4


def _ffn_kernel(*refs, final_norm):
    if final_norm:
        h_ref, hp_ref, hn_ref, mod_ref, g2_ref, wu_ref, cw_ref, cb_ref, wd_ref, gf_ref, o_ref = refs
    else:
        h_ref, hp_ref, hn_ref, mod_ref, g2_ref, wu_ref, cw_ref, cb_ref, wd_ref, o_ref = refs
    i, n_tiles = pl.program_id(1), pl.num_programs(1)
    x = h_ref[...]
    tm, d = x.shape
    d_ff = wd_ref.shape[0]
    shift, scale, g2 = mod_ref[3:4, :], mod_ref[4:5, :], g2_ref[...]
    u_prev = jnp.where(i > 0, _norm_mod(hp_ref[...], g2, shift, scale), 0.0)
    u_next = jnp.where(i < n_tiles - 1, _norm_mod(hn_ref[...], g2, shift, scale), 0.0)
    u = jnp.concatenate([u_prev, _norm_mod(x, g2, shift, scale), u_next], axis=0).astype(MXU_DTYPE)
    n_ext = tm + 2 * SUBLANES

    def conv(a, lo, width):
        w = cw_ref[:, lo:lo + width]
        prev = pltpu.roll(a, 1, axis=0)[SUBLANES:SUBLANES + tm]
        nxt = pltpu.roll(a, n_ext - 1, axis=0)[SUBLANES:SUBLANES + tm]
        return (w[0:1] * prev + w[1:2] * a[SUBLANES:SUBLANES + tm] + w[2:3] * nxt) + cb_ref[:, lo:lo + width]

    def up(lo):
        width = min(FF_CHUNK, d_ff - lo)
        return (jnp.dot(u, wu_ref[:, lo:lo + width], preferred_element_type=F32),
                jnp.dot(u, wu_ref[:, d_ff + lo:d_ff + lo + width], preferred_element_type=F32))

    chunks = list(range(0, d_ff, FF_CHUNK))
    acc, ahead = None, [up(lo) for lo in chunks[:FF_AHEAD]]
    for c, lo in enumerate(chunks):
        width = min(FF_CHUNK, d_ff - lo)
        a_gate, a_val = ahead.pop(0)
        if c + FF_AHEAD < len(chunks):
            ahead.append(up(chunks[c + FF_AHEAD]))
        gate = conv(a_gate, lo, width)
        val = conv(a_val, d_ff + lo, width)
        z = gate * _sigmoid(gate) * val
        part = _mm(z, wd_ref[lo:lo + width, :])
        acc = part if acc is None else acc + part
    y = x + mod_ref[5:6, :] * acc
    if final_norm:
        y = _rms(y, gf_ref[...])
    o_ref[...] = y


def _ffn(h, mod_l, mod_row0, wts, tm, g_final=None):
    nb, ns, d = h.shape
    d_ff = wts["w_down"].shape[0]
    per = tm // SUBLANES
    last = ns // SUBLANES - 1
    tok = pl.BlockSpec((None, tm, d), lambda b, i: (b, i, 0))
    in_specs = [
        tok,
        pl.BlockSpec((None, SUBLANES, d), lambda b, i: (b, jnp.maximum(i * per - 1, 0), 0)),
        pl.BlockSpec((None, SUBLANES, d), lambda b, i: (b, jnp.minimum((i + 1) * per, last), 0)),
        pl.BlockSpec((None, N_MOD, d), (lambda b, i: (b + 1, 0, 0)) if mod_row0 else (lambda b, i: (0, 0, 0))),
        _resident((1, d)), _resident((d, 2 * d_ff)), _resident((CONV_W, 2 * d_ff)), _resident((1, 2 * d_ff)),
        _resident((d_ff, d)),
    ]
    args = [h, h, h, mod_l, wts["g2"], wts["w_up"], wts["conv_w"], wts["conv_b"], wts["w_down"]]
    if g_final is not None:
        in_specs.append(_resident((1, d)))
        args.append(g_final)
    return pl.pallas_call(
        functools.partial(_ffn_kernel, final_norm=g_final is not None),
        out_shape=jax.ShapeDtypeStruct(h.shape, F32),
        grid=(nb, ns // tm),
        in_specs=in_specs,
        out_specs=tok,
        compiler_params=_params(2),
        name="conv_ffn",
    )(*args)


def _layer_weights(l, g_norm1, w_in, lam, g_diff, g_cq, w_uq, g_ckv, w_ukv, sink, w_branch, w_gate, b_gate,
                   w_o, g_norm2, w_up, conv_w, conv_b, w_down):
    mx = lambda a: a.astype(MXU_DTYPE)
    row = lambda a: a.reshape(1, -1).astype(F32)
    wb = w_branch[l]
    row_src = _wbranch_c_src()
    wb_c = jnp.concatenate([wb[2, int(r):int(r) + DC] for r in row_src[::DC]], axis=0)
    wb = jnp.stack([wb[0], wb[1], wb_c])
    return {
        "g1": row(g_norm1[l]), "w_in": mx(_take_cols(w_in[l], _in_proj_src())),
        "lam": lam[l].astype(F32), "g_diff": row(g_diff[l]),
        "g_cq": row(g_cq[l]), "w_uq": mx(_take_cols(w_uq[l], _wuq_src())),
        "g_ckv": row(g_ckv[l]), "w_kv": mx(_take_cols(w_ukv[l], _wkv_src())),
        "sink": sink[l].astype(F32),
        "w_gate": mx(w_gate[l]), "b_gate": row(b_gate[l]), "w_branch": mx(wb), "w_o": mx(w_o[l]),
        "g2": row(g_norm2[l]), "w_up": mx(w_up[l]), "conv_w": conv_w[l].astype(F32), "conv_b": row(conv_b[l]),
        "w_down": mx(w_down[l]),
    }


def _tile(n, want):
    t = min(n, want)
    assert n % t == 0, (n, t)
    return t


def kernel(x, c, ctx, c_ctx, w_ada, b_ada, g_norm1, w_in, lam, g_diff, g_cq, w_uq, g_ckv, w_ukv, sink, w_branch,
           w_gate, b_gate, w_o, g_norm2, w_up, conv_w, conv_b, w_down, g_final):
    n_batch, n_lat, d = x.shape
    n_ctx = ctx.shape[1]
    n_layers = w_ada.shape[0]
    assert n_lat % GRID_W == 0 and n_lat % WINDOW == 0
    tm_lat, tm_ctx = _tile(n_lat, 512), _tile(n_ctx, 256)
    tm_wide = _tile(n_lat, 1024)
    tq_lat, tq_ctx = _tile(n_lat, 256), _tile(n_ctx, 256)
    assert n_lat >= tq_lat + 2 * WINDOW

    n_rows = -(-(1 + n_batch) // SUBLANES) * SUBLANES
    c_all = jnp.concatenate([c_ctx[None, :], c, jnp.zeros((n_rows - 1 - n_batch, d), c.dtype)], axis=0)
    mod = _modulation(c_all.astype(F32), w_ada, b_ada).reshape(n_layers, n_rows, N_MOD, d)
    tables = _rope_tables(n_lat)

    h, hc = x.astype(F32), ctx.astype(F32)
    for l in range(n_layers):
        ctx_out = l < n_layers - 1
        lam_init = 0.8 - 0.6 * math.exp(-0.3 * l)
        wts = _layer_weights(l, g_norm1, w_in, lam, g_diff, g_cq, w_uq, g_ckv, w_ukv, sink, w_branch, w_gate,
                             b_gate, w_o, g_norm2, w_up, conv_w, conv_b, w_down)
        lat = _inproj(h, mod[l], True, wts, tables, tm_wide, True)
        cx = _inproj(hc, mod[l], False, wts, None, tm_ctx, ctx_out)
        ys = _mixers(lat, cx, lat, wts, lam_init, tq_lat)
        h = _merge(h, mod[l], True, wts, ys, tm_wide)
        h = _ffn(h, mod[l], True, wts, tm_lat, g_final.reshape(1, d).astype(F32) if not ctx_out else None)
        if ctx_out:
            ys_c = _mixers(cx, cx, None, wts, lam_init, tq_ctx)
            hc = _merge(hc, mod[l], False, wts, ys_c, tm_ctx)
            hc = _ffn(hc, mod[l], False, wts, tm_ctx)
    return h
```

```python
import functools
import math

import numpy as np
import jax
import jax.numpy as jnp
from jax import lax
from jax.experimental import pallas as pl
from jax.experimental.pallas import tpu as pltpu

F32 = jnp.float32
MXU_DTYPE = jnp.bfloat16

GRID_W = 64
HA, DA = 4, 64
HB, NOPE_B, ROPE_B, V_B = 8, 64, 32, 64
Q_RANK, KV_RANK = 256, 128
HC, KVC, DC = 8, 2, 64
G_C = HC // KVC
WINDOW = 128
BRANCH_W = 512
N_BRANCH = 3
CONV_W = 3
ROPE_BASE = 10000.0
EPS = 1e-6
NEG_INF = -1e30
N_MOD = 6
MLA_SCALE = (NOPE_B + ROPE_B) ** -0.5
IN_WIDTHS = (HA * 2 * DA, HA * 2 * DA, HA * 2 * DA, Q_RANK, KV_RANK, ROPE_B, HC * DC, KVC * DC, KVC * DC)
IN_OFFS = tuple(int(v) for v in np.cumsum((0,) + IN_WIDTHS))

LANES = 128
SUBLANES = 8
HALF = LANES // 2
VMEM_LIMIT = 56 * 1024 * 1024

SEG_QA, SEG_KA, SEG_VA, SEG_QC = 0, 512, 1024, 1536
SEG_KC, SEG_VC, SEG_CQ, SEG_CKV, SEG_KR = 2048, 2176, 2432, 2688, 2816
NP_IN = 2944
LOG2E = 1.4426950408889634
KEY_BLOCK = 512
SCORE_SLOTS = {"a": 2, "b": 2, "c": 3}
SUB_TILES = {"a": 2, "b": 1, "c": 4}

def _slab64_src(base, unit_of_group):
    src = np.zeros(LANES, np.int64)
    for grp in range(4):
        unit, half = grp % 2, grp // 2
        for r in range(32):
            src[grp * 32 + r] = base + unit_of_group[unit] + half * 32 + r
    return src


def _in_proj_src():
    src = -np.ones(NP_IN, np.int64)
    qa0, ka0, va0, cq0, ckv0, kr0, qc0, kc0, vc0 = IN_OFFS[:9]
    for h in range(HA):
        src[SEG_QA + h * LANES:SEG_QA + (h + 1) * LANES] = _slab64_src(qa0 + h * 2 * DA, (0, DA))
        src[SEG_KA + h * LANES:SEG_KA + (h + 1) * LANES] = _slab64_src(ka0 + h * 2 * DA, (0, DA))
    src[SEG_VA:SEG_VA + 512] = va0 + np.arange(512)
    for g in range(G_C):
        src[SEG_QC + g * LANES:SEG_QC + (g + 1) * LANES] = _slab64_src(qc0, (g * DC, (G_C + g) * DC))
    src[SEG_KC:SEG_KC + LANES] = _slab64_src(kc0, (0, DC))
    for kv in range(KVC):
        v_lo = SEG_VC + kv * LANES + (kv % 2) * HALF
        src[v_lo:v_lo + DC] = vc0 + kv * DC + np.arange(DC)
    src[SEG_CQ:SEG_CQ + Q_RANK] = cq0 + np.arange(Q_RANK)
    src[SEG_CKV:SEG_CKV + KV_RANK] = ckv0 + np.arange(KV_RANK)
    src[SEG_KR:SEG_KR + 16] = kr0 + np.arange(16)
    src[SEG_KR + HALF:SEG_KR + HALF + 16] = kr0 + 16 + np.arange(16)
    return src


def _mla_slab_src(nope0, rope0):
    src = -np.ones(LANES, np.int64)
    if rope0 is not None:
        src[0:16] = rope0 + np.arange(16)
        src[HALF:HALF + 16] = rope0 + 16 + np.arange(16)
    src[16:64] = nope0 + np.arange(48)
    src[80:96] = nope0 + 48 + np.arange(16)
    return src


def _wuq_src():
    src = -np.ones(HB * LANES, np.int64)
    for h in range(HB):
        b0 = h * (NOPE_B + ROPE_B)
        src[h * LANES:(h + 1) * LANES] = _mla_slab_src(b0, b0 + NOPE_B)
    return src


def _wkv_src():
    src = -np.ones(2 * HB * LANES, np.int64)
    for h in range(HB):
        b0 = h * (NOPE_B + V_B)
        src[h * LANES:(h + 1) * LANES] = _mla_slab_src(b0, None)
        v_lo = HB * LANES + h * LANES + (h % 2) * HALF
        src[v_lo:v_lo + V_B] = b0 + NOPE_B + np.arange(V_B)
    return src


def _wbranch_c_src():
    src = np.zeros(BRANCH_W, np.int64)
    for g in range(G_C):
        for kv in range(KVC):
            src[g * LANES + kv * DC:g * LANES + (kv + 1) * DC] = (kv * G_C + g) * DC + np.arange(DC)
    return src


def _take_cols(w, src):
    parts, start = [], 0
    for i in range(1, len(src) + 1):
        if i == len(src) or (src[i] != src[i - 1] + 1 if src[i - 1] >= 0 else src[i] >= 0):
            if src[start] < 0:
                parts.append(jnp.zeros(w.shape[:-1] + (i - start,), w.dtype))
            else:
                parts.append(w[..., int(src[start]):int(src[start]) + i - start])
            start = i
    return jnp.concatenate(parts, axis=-1)


def _rope_tables(n_tok):
    def table(rot_dim):
        n_freq = rot_dim // 4
        inv = ROPE_BASE ** (-jnp.arange(n_freq, dtype=F32) / n_freq)
        n_rows = n_tok // GRID_W
        rows = jnp.repeat(jnp.arange(n_rows, dtype=F32), GRID_W)
        cols = jnp.tile(jnp.arange(GRID_W, dtype=F32), n_rows)
        ang = jnp.concatenate([rows[:, None] * inv, cols[:, None] * inv], axis=-1)
        return jnp.cos(ang), jnp.sin(ang)

    cos64, sin64 = table(DA)
    cosb, sinb = table(ROPE_B)
    cos64_t = jnp.concatenate([cos64] * 4, axis=-1)
    sin64_t = jnp.concatenate([-sin64, -sin64, sin64, sin64], axis=-1)
    ones = jnp.ones((n_tok, HALF - 16), F32)
    zeros = jnp.zeros((n_tok, HALF - 16), F32)
    cosb_t = jnp.concatenate([cosb, ones, cosb, ones], axis=-1)
    sinb_t = jnp.concatenate([-sinb, zeros, sinb, zeros], axis=-1)
    return cos64_t, sin64_t, cosb_t, sinb_t


def _mm(a, b):
    return jnp.dot(a.astype(MXU_DTYPE), b.astype(MXU_DTYPE), preferred_element_type=F32)


def _mm_t(a, b):
    return lax.dot_general(a.astype(MXU_DTYPE), b.astype(MXU_DTYPE), (((1,), (1,)), ((), ())),
                           preferred_element_type=F32)


def _rms(x, g):
    return x * lax.rsqrt(jnp.mean(x * x, axis=-1, keepdims=True) + EPS) * g


def _norm_mod(x, g, shift, scale):
    return _rms(x, g) * (1.0 + scale) + shift


def _sigmoid(x):
    return 1.0 / (1.0 + jnp.exp(-x))


def _rope_slab(x, cos_t, sin_t):
    return x * cos_t + pltpu.roll(x, HALF, axis=1) * sin_t


def _ones_lanes(n_slab):
    lane = lax.broadcasted_iota(jnp.int32, (1, n_slab * LANES), 1)
    slab_parity = (lane // LANES) % 2
    upper_half = (lane % LANES) // HALF
    return jnp.where(slab_parity + upper_half == 1, 1.0, 0.0)


def _normalise_half(o, extra=None):
    den = o if extra is None else o + extra
    return o * pltpu.roll(1.0 / den, HALF, axis=1)


def _resident(shape):
    nd = len(shape)
    return pl.BlockSpec(shape, lambda *_: (0,) * nd, pipeline_mode=pl.Buffered(1))


def _params(n_grid):
    return pltpu.CompilerParams(dimension_semantics=("arbitrary",) * n_grid, vmem_limit_bytes=VMEM_LIMIT)


def _mod_kernel(c_ref, w_ref, b_ref, o_ref):
    c = c_ref[...]
    s = c * _sigmoid(c)
    o_ref[...] = _mm(s, w_ref[...]) + b_ref[...]


def _modulation(c_all, w_ada, b_ada):
    n_layers, d, n_out = w_ada.shape
    rows = c_all.shape[0]
    tn = 1536
    return pl.pallas_call(
        _mod_kernel,
        out_shape=jax.ShapeDtypeStruct((n_layers, rows, n_out), F32),
        grid=(n_layers, n_out // tn),
        in_specs=[
            pl.BlockSpec((rows, d), lambda l, j: (0, 0)),
            pl.BlockSpec((None, d, tn), lambda l, j: (l, 0, j)),
            pl.BlockSpec((None, 1, tn), lambda l, j: (l, 0, j)),
        ],
        out_specs=pl.BlockSpec((None, rows, tn), lambda l, j: (l, 0, j)),
        compiler_params=_params(2),
        name="modulation",
    )(c_all, w_ada, b_ada.reshape(n_layers, 1, n_out))


def _inproj_kernel(*refs, rope, need_q):
    it = iter(refs)
    h_ref, mod_ref, g1_ref, w_ref, gcq_ref, wuq_ref, gckv_ref, wkv_ref = (next(it) for _ in range(8))
    if rope:
        cos64_ref, sin64_ref, cosb_ref, sinb_ref = (next(it) for _ in range(4))
    if need_q:
        qa_ref, qb_ref, qc_ref = (next(it) for _ in range(3))
    ka_ref, va_ref, kb_ref, vb_ref, kc_ref, vc_ref = (next(it) for _ in range(6))

    u = _norm_mod(h_ref[...], g1_ref[...], mod_ref[0:1, :], mod_ref[1:2, :]).astype(MXU_DTYPE)

    def proj(lo, width):
        return jnp.dot(u, w_ref[:, lo:lo + width], preferred_element_type=F32)

    def rot64(x):
        return _rope_slab(x, cos64_ref[...], sin64_ref[...]) if rope else x

    def rotb(x):
        return _rope_slab(x, cosb_ref[...], sinb_ref[...]) if rope else x

    def store_rot64(dst_ref, lo, n_slab, scale):
        p = proj(lo, n_slab * LANES)
        for s in range(n_slab):
            y = rot64(p[:, s * LANES:(s + 1) * LANES])
            if scale != 1.0:
                y = y * scale
            dst_ref[:, s * LANES:(s + 1) * LANES] = y.astype(dst_ref.dtype)

    low = proj(SEG_CQ, Q_RANK + KV_RANK + LANES)
    if need_q:
        store_rot64(qa_ref, SEG_QA, HA, DA ** -0.5 * LOG2E)
        store_rot64(qc_ref, SEG_QC, G_C, DC ** -0.5 * LOG2E)
    store_rot64(ka_ref, SEG_KA, HA, 1.0)
    va_ref[...] = proj(SEG_VA, 512).astype(va_ref.dtype)
    kcvc = proj(SEG_KC, (1 + KVC) * LANES)
    kc_ref[...] = rot64(kcvc[:, :LANES]).astype(kc_ref.dtype)
    vc_ref[...] = (kcvc[:, LANES:] + _ones_lanes(KVC)).astype(vc_ref.dtype)

    if need_q:
        cq = _rms(low[:, :Q_RANK], gcq_ref[...])
        qh = _mm(cq, wuq_ref[...])
        for h in range(HB):
            y = rotb(qh[:, h * LANES:(h + 1) * LANES]) * (MLA_SCALE * LOG2E)
            qb_ref[:, h * LANES:(h + 1) * LANES] = y.astype(qb_ref.dtype)
    ckv = _rms(low[:, Q_RANK:Q_RANK + KV_RANK], gckv_ref[...])
    kvh = _mm(ckv, wkv_ref[...])
    kr = rotb(low[:, Q_RANK + KV_RANK:])
    for h in range(HB):
        kb_ref[:, h * LANES:(h + 1) * LANES] = (kvh[:, h * LANES:(h + 1) * LANES] + kr).astype(kb_ref.dtype)
    vb_ref[...] = (kvh[:, HB * LANES:] + _ones_lanes(HB)).astype(vb_ref.dtype)


def _inproj(h, mod_l, mod_row0, wts, tables, tm, need_q):
    nb, ns, d = h.shape
    rope = tables is not None
    tok = lambda w: pl.BlockSpec((None, tm, w), lambda b, i: (b, i, 0))
    in_specs = [
        tok(d),
        pl.BlockSpec((None, N_MOD, d), (lambda b, i: (b + 1, 0, 0)) if mod_row0 else (lambda b, i: (0, 0, 0))),
        _resident((1, d)), _resident((d, NP_IN)),
        _resident((1, Q_RANK)), _resident((Q_RANK, HB * LANES)),
        _resident((1, KV_RANK)), _resident((KV_RANK, 2 * HB * LANES)),
    ]
    args = [h, mod_l, wts["g1"], wts["w_in"], wts["g_cq"], wts["w_uq"], wts["g_ckv"], wts["w_kv"]]
    if rope:
        in_specs += [pl.BlockSpec((tm, LANES), lambda b, i: (i, 0))] * 4
        args += list(tables)
    names, widths = [], []
    if need_q:
        names += ["qa", "qb", "qc"]
        widths += [512, HB * LANES, 512]
    names += ["ka", "va", "kb", "vb", "kc", "vc"]
    widths += [512, 512, HB * LANES, HB * LANES, LANES, KVC * LANES]
    outs = pl.pallas_call(
        functools.partial(_inproj_kernel, rope=rope, need_q=need_q),
        out_shape=[jax.ShapeDtypeStruct((nb, ns, w), MXU_DTYPE) for w in widths],
        grid=(nb, ns // tm),
        in_specs=in_specs,
        out_specs=[tok(w) for w in widths],
        compiler_params=_params(2),
        name="inproj",
    )(*args)
    return dict(zip(names, outs))


def _score_pass(q, kv_blocks, s_ref, m_ref, m_extra=None):
    n_rows = q.shape[0]
    col = 0
    for idx, (load_k, _, ok, n) in enumerate(kv_blocks):
        s = _mm_t(q, load_k())
        if ok is not None:
            s = jnp.where(ok, s, NEG_INF)
        s_ref[:, col:col + n] = s
        block_max = functools.reduce(jnp.maximum, [s[:, c * LANES:(c + 1) * LANES] for c in range(n // LANES)])
        m_ref[...] = block_max if idx == 0 else jnp.maximum(m_ref[...], block_max)
        col += n
        yield
    m = jnp.max(m_ref[...], axis=-1, keepdims=True)
    if m_extra is not None:
        m = jnp.maximum(m, m_extra)
    m_ref[...] = jnp.broadcast_to(m, (n_rows, LANES))
    return m


def _value_pass(kv_blocks, s_ref, m_ref, l_ref=None):
    col, o = 0, None
    for idx, (_, load_v, _, n) in enumerate(kv_blocks):
        ps = [jnp.exp2(s_ref[:, col + c * LANES:col + (c + 1) * LANES] - m_ref[...]) for c in range(n // LANES)]
        if l_ref is not None:
            block_sum = functools.reduce(lambda a, b: a + b, ps)
            l_ref[...] = block_sum if idx == 0 else l_ref[...] + block_sum
        pv = _mm(jnp.concatenate(ps, axis=-1), load_v())
        o = pv if o is None else o + pv
        col += n
        yield
    l = jnp.sum(l_ref[...], axis=-1, keepdims=True) if l_ref is not None else None
    return o, l


def _step_together(gens):
    results, live = [None] * len(gens), list(range(len(gens)))
    while live:
        for i in list(live):
            try:
                next(gens[i])
            except StopIteration as stop:
                results[i] = stop.value
                live.remove(i)
    return results


def _run_units(units):
    n_stage = len(units[0])
    done = {}
    for t in range(len(units) + n_stage - 1):
        owners = [(t - k, k) for k in range(n_stage) if 0 <= t - k < len(units)]
        gens = [units[i][k]() if k == 0 else units[i][k](done.pop((i, k - 1))) for i, k in owners]
        for owner, result in zip(owners, _step_together(gens)):
            done[owner] = result


def _hold(state):
    return state
    yield


def _key_blocks(kc_ref, vc_ref, kl_ref, vl_ref, k_sl, v_sl):
    blocks = [(lambda: kc_ref[:, k_sl], lambda: vc_ref[:, v_sl], None, kc_ref.shape[0])]
    if kl_ref is not None:
        n_lat = kl_ref.shape[0]
        kb = min(KEY_BLOCK, n_lat)
        for j in range(n_lat // kb):
            rows = slice(j * kb, (j + 1) * kb)
            blocks.append((lambda rows=rows: kl_ref[rows, k_sl], lambda rows=rows: vl_ref[rows, v_sl], None, kb))
    return blocks


def _unit_mask(unit):
    lane = lax.broadcasted_iota(jnp.int32, (1, LANES), 1)
    return jnp.where((lane // 32) % 2 == unit, 1.0, 0.0).astype(MXU_DTYPE)


def _units_a(lam_ref, gd_ref, q_ref, kc_ref, vc_ref, kl_ref, vl_ref, o_ref, s_ref, m_ref, lam_init):
    def ones_slab(n):
        return jnp.ones((n, LANES), MXU_DTYPE)
    tq = s_ref.shape[1] // 2
    lp = lam_ref[...]
    lam = (jnp.exp(jnp.sum(lp[0:1] * lp[1:2], axis=-1, keepdims=True))
           - jnp.exp(jnp.sum(lp[2:3] * lp[3:4], axis=-1, keepdims=True)) + lam_init)
    m0, m1 = _unit_mask(0), _unit_mask(1)

    def unit(j, h):
        sl = slice(h * LANES, (h + 1) * LANES)
        rows = slice(j * tq, (j + 1) * tq)
        slot = (j * HA + h) % s_ref.shape[0]
        blocks = [(load_k, lambda load_v=load_v, n=n: jnp.concatenate([load_v(), ones_slab(n)], axis=1), ok, n)
                  for load_k, load_v, ok, n in _key_blocks(kc_ref, vc_ref, kl_ref, vl_ref, sl, sl)]
        s_h, m_h = s_ref.at[slot], m_ref.at[slot]

        def score():
            q = q_ref[rows, sl]
            qq = jnp.concatenate([q * m0, q * m1], axis=0)
            yield from _score_pass(qq, blocks, s_h, m_h)

        def finish(_):
            o_ext, _ = yield from _value_pass(blocks, s_h, m_h)
            o2 = o_ext[:, :LANES]
            r = 1.0 / o_ext[:, LANES:]
            o = o2[:tq] * r[:tq] - o2[tq:] * (lam * r[tq:])
            o = _rms(o, gd_ref[...]) * (1.0 - lam_init)
            o_ref[rows, sl] = o.astype(o_ref.dtype)

        return (score,) + (_hold,) * (s_ref.shape[0] - 2) + (finish,)

    return [unit(j, h) for j in range(q_ref.shape[0] // tq) for h in range(HA)]


def _attn_b_kernel(*refs, has_lat):
    if has_lat:
        q_ref, kc_ref, vc_ref, kl_ref, vl_ref, o_ref, s_ref, m_ref = refs
    else:
        q_ref, kc_ref, vc_ref, o_ref, s_ref, m_ref = refs
        kl_ref = vl_ref = None
    lane = lax.broadcasted_iota(jnp.int32, (1, LANES), 1)
    o_even = {}

    def unit(h):
        sl = slice(h * LANES, (h + 1) * LANES)
        slot = h % s_ref.shape[0]
        blocks = _key_blocks(kc_ref, vc_ref, kl_ref, vl_ref, sl, sl)
        s_h, m_h = s_ref.at[slot], m_ref.at[slot]

        def finish(_):
            o, _ = yield from _value_pass(blocks, s_h, m_h)
            o = _normalise_half(o)
            if h % 2 == 0:
                o_even[h // 2] = o
            else:
                pair = h // 2
                o_ref[:, pair * LANES:(pair + 1) * LANES] = jnp.where(lane < HALF, o_even.pop(pair), o).astype(o_ref.dtype)

        score = lambda: _score_pass(q_ref[:, sl], blocks, s_h, m_h)
        return (score,) + (_hold,) * (s_ref.shape[0] - 2) + (finish,)

    _run_units([unit(h) for h in range(HB)])


def _units_c(sink_ref, q_ref, kc_ref, vc_ref, kl_ref, vl_ref, o_ref, s_ref, m_ref, n_lat):
    has_lat = kl_ref is not None
    tq = s_ref.shape[1]
    n_sub = q_ref.shape[0] // tq
    lane = lax.broadcasted_iota(jnp.int32, (1, LANES), 1)
    masks = (_unit_mask(0), _unit_mask(1))
    o_kv0 = {}

    def window(j):
        band = tq + 2 * WINDOW
        q0 = pl.program_id(1) * q_ref.shape[0] + j * tq
        start = pl.multiple_of(jnp.minimum(jnp.maximum(q0 - WINDOW, 0), n_lat - band), WINDOW)
        qpos = q0 + lax.broadcasted_iota(jnp.int32, (tq, 1), 0)
        kpos = start + lax.broadcasted_iota(jnp.int32, (1, band), 1)
        return start, band, jnp.abs(qpos - kpos) <= WINDOW

    windows = [window(j) for j in range(n_sub)] if has_lat else None

    def unit(j, g, kv):
        sl = slice(g * LANES, (g + 1) * LANES)
        v_sl = slice(kv * LANES, (kv + 1) * LANES)
        rows = slice(j * tq, (j + 1) * tq)
        blocks = [(lambda: kc_ref[...], lambda: vc_ref[:, v_sl], None, kc_ref.shape[0])]
        if has_lat:
            start, band, ok = windows[j]
            blocks.append((lambda: kl_ref[pl.ds(start, band), :], lambda: vl_ref[pl.ds(start, band), v_sl], ok, band))
        slot = ((j * G_C + g) * KVC + kv) % s_ref.shape[0]
        s_u, m_u = s_ref.at[slot], m_ref.at[slot]

        def score():
            sink = jnp.full((1, 1), sink_ref[kv * G_C + g] * LOG2E, F32)
            yield from _score_pass(q_ref[rows, sl] * masks[kv], blocks, s_u, m_u, m_extra=sink)
            return sink

        def finish(sink):
            o, _ = yield from _value_pass(blocks, s_u, m_u)
            o = _normalise_half(o, extra=jnp.exp2(sink - m_u[...]))
            if kv == 0:
                o_kv0[(j, g)] = o
            else:
                o_ref[rows, sl] = jnp.where(lane < HALF, o_kv0.pop((j, g)), o).astype(o_ref.dtype)

        return (score,) + (_hold,) * (s_ref.shape[0] - 2) + (finish,)

    return [unit(j, g, kv) for j in range(n_sub) for g in range(G_C) for kv in range(KVC)]


def _attn_a_kernel(*refs, lam_init, has_lat):
    refs = list(refs)
    if not has_lat:
        refs[5:5] = [None, None]
    _run_units(_units_a(*refs, lam_init))


def _attn_c_kernel(*refs, n_lat, has_lat):
    refs = list(refs)
    if not has_lat:
        refs[4:4] = [None, None]
    _run_units(_units_c(*refs, n_lat))


def _attention(kind, q, kv_ctx, kv_lat, tq, extra_args, extra_specs, **static):
    nb, nq, qw = q.shape
    has_lat = kv_lat is not None
    full = lambda a: pl.BlockSpec((None,) + a.shape[1:], lambda b, i: (b, 0, 0))
    tq_block = tq * SUB_TILES[kind] if nq % (tq * SUB_TILES[kind]) == 0 else tq
    in_specs = list(extra_specs) + [pl.BlockSpec((None, tq_block, qw), lambda b, i: (b, i, 0))]
    in_specs += [full(kv_ctx[0]), full(kv_ctx[1])]
    args = list(extra_args) + [q, kv_ctx[0], kv_ctx[1]]
    n_keys = kv_ctx[0].shape[1]
    if has_lat:
        in_specs += [full(kv_lat[0]), full(kv_lat[1])]
        args += [kv_lat[0], kv_lat[1]]
        n_keys += (tq + 2 * WINDOW) if kind == "c" else kv_lat[0].shape[1]
    n_rows = 2 * tq if kind == "a" else tq
    n_slots = SCORE_SLOTS[kind]
    body = {"a": _attn_a_kernel, "b": _attn_b_kernel, "c": _attn_c_kernel}[kind]
    return pl.pallas_call(
        functools.partial(body, has_lat=has_lat, **static),
        out_shape=jax.ShapeDtypeStruct((nb, nq, BRANCH_W), MXU_DTYPE),
        grid=(nb, nq // tq_block),
        in_specs=in_specs,
        out_specs=pl.BlockSpec((None, tq_block, BRANCH_W), lambda b, i: (b, i, 0)),
        scratch_shapes=[pltpu.VMEM((n_slots, n_rows, n_keys), F32), pltpu.VMEM((n_slots, n_rows, LANES), F32)],
        compiler_params=_params(2),
        name="attn_" + kind,
    )(*args)


def _mixers(qs, ctx_kv, lat_kv, wts, lam_init, tq):
    pick = lambda d, k, v: (d[k], d[v]) if d is not None else None
    ya = _attention("a", qs["qa"], pick(ctx_kv, "ka", "va"), pick(lat_kv, "ka", "va"), tq,
                    [wts["lam"], wts["g_diff"]], [_resident((4, DA)), _resident((1, 2 * DA))], lam_init=lam_init)
    n_q = qs["qb"].shape[1]
    tq_b = 2 * tq if n_q % (2 * tq) == 0 else tq
    yb = _attention("b", qs["qb"], pick(ctx_kv, "kb", "vb"), pick(lat_kv, "kb", "vb"), tq_b, [], [])
    n_lat = lat_kv["kc"].shape[1] if lat_kv is not None else 0
    yc = _attention("c", qs["qc"], pick(ctx_kv, "kc", "vc"), pick(lat_kv, "kc", "vc"), tq,
                    [wts["sink"]], [pl.BlockSpec(memory_space=pltpu.SMEM)], n_lat=n_lat)
    return ya, yb, yc


def _merge_kernel(h_ref, mod_ref, g1_ref, ya_ref, yb_ref, yc_ref, wg_ref, bg_ref, wb_ref, wo_ref, o_ref):
    x = h_ref[...]
    d = x.shape[-1]
    u = _norm_mod(x, g1_ref[...], mod_ref[0:1, :], mod_ref[1:2, :]).astype(MXU_DTYPE)
    acc = None
    for n, y_ref in enumerate((ya_ref, yb_ref, yc_ref)):
        gate = _sigmoid(jnp.dot(u, wg_ref[:, n * d:(n + 1) * d], preferred_element_type=F32)
                        + bg_ref[:, n * d:(n + 1) * d])
        term = gate * jnp.dot(y_ref[...], wb_ref[n], preferred_element_type=F32)
        acc = term if acc is None else acc + term
    o_ref[...] = x + mod_ref[2:3, :] * _mm(acc, wo_ref[...])


def _merge(h, mod_l, mod_row0, wts, ys, tm):
    nb, ns, d = h.shape
    tok = lambda w: pl.BlockSpec((None, tm, w), lambda b, i: (b, i, 0))
    return pl.pallas_call(
        _merge_kernel,
        out_shape=jax.ShapeDtypeStruct(h.shape, F32),
        grid=(nb, ns // tm),
        in_specs=[
            tok(d),
            pl.BlockSpec((None, N_MOD, d), (lambda b, i: (b + 1, 0, 0)) if mod_row0 else (lambda b, i: (0, 0, 0))),
            _resident((1, d)), tok(BRANCH_W), tok(BRANCH_W), tok(BRANCH_W),
            _resident((d, N_BRANCH * d)), _resident((1, N_BRANCH * d)),
            _resident((N_BRANCH, BRANCH_W, d)), _resident((d, d)),
        ],
        out_specs=tok(d),
        compiler_params=_params(2),
        name="merge",
    )(h, mod_l, wts["g1"], *ys, wts["w_gate"], wts["b_gate"], wts["w_branch"], wts["w_o"])


FF_CHUNK = 512
FF_AHEAD = 4


def _ffn_kernel(*refs, final_norm):
    if final_norm:
        h_ref, hp_ref, hn_ref, mod_ref, g2_ref, wu_ref, cw_ref, cb_ref, wd_ref, gf_ref, o_ref = refs
    else:
        h_ref, hp_ref, hn_ref, mod_ref, g2_ref, wu_ref, cw_ref, cb_ref, wd_ref, o_ref = refs
    i, n_tiles = pl.program_id(1), pl.num_programs(1)
    x = h_ref[...]
    tm, d = x.shape
    d_ff = wd_ref.shape[0]
    shift, scale, g2 = mod_ref[3:4, :], mod_ref[4:5, :], g2_ref[...]
    u_prev = jnp.where(i > 0, _norm_mod(hp_ref[...], g2, shift, scale), 0.0)
    u_next = jnp.where(i < n_tiles - 1, _norm_mod(hn_ref[...], g2, shift, scale), 0.0)
    u = jnp.concatenate([u_prev, _norm_mod(x, g2, shift, scale), u_next], axis=0).astype(MXU_DTYPE)
    n_ext = tm + 2 * SUBLANES

    def conv(a, lo, width):
        w = cw_ref[:, lo:lo + width]
        prev = pltpu.roll(a, 1, axis=0)[SUBLANES:SUBLANES + tm]
        nxt = pltpu.roll(a, n_ext - 1, axis=0)[SUBLANES:SUBLANES + tm]
        return (w[0:1] * prev + w[1:2] * a[SUBLANES:SUBLANES + tm] + w[2:3] * nxt) + cb_ref[:, lo:lo + width]

    def up(lo):
        width = min(FF_CHUNK, d_ff - lo)
        return (jnp.dot(u, wu_ref[:, lo:lo + width], preferred_element_type=F32),
                jnp.dot(u, wu_ref[:, d_ff + lo:d_ff + lo + width], preferred_element_type=F32))

    chunks = list(range(0, d_ff, FF_CHUNK))
    acc, ahead = None, [up(lo) for lo in chunks[:FF_AHEAD]]
    for c, lo in enumerate(chunks):
        width = min(FF_CHUNK, d_ff - lo)
        a_gate, a_val = ahead.pop(0)
        if c + FF_AHEAD < len(chunks):
            ahead.append(up(chunks[c + FF_AHEAD]))
        gate = conv(a_gate, lo, width)
        val = conv(a_val, d_ff + lo, width)
        z = gate * _sigmoid(gate) * val
        part = _mm(z, wd_ref[lo:lo + width, :])
        acc = part if acc is None else acc + part
    y = x + mod_ref[5:6, :] * acc
    if final_norm:
        y = _rms(y, gf_ref[...])
    o_ref[...] = y


def _ffn(h, mod_l, mod_row0, wts, tm, g_final=None):
    nb, ns, d = h.shape
    d_ff = wts["w_down"].shape[0]
    per = tm // SUBLANES
    last = ns // SUBLANES - 1
    tok = pl.BlockSpec((None, tm, d), lambda b, i: (b, i, 0))
    in_specs = [
        tok,
        pl.BlockSpec((None, SUBLANES, d), lambda b, i: (b, jnp.maximum(i * per - 1, 0), 0)),
        pl.BlockSpec((None, SUBLANES, d), lambda b, i: (b, jnp.minimum((i + 1) * per, last), 0)),
        pl.BlockSpec((None, N_MOD, d), (lambda b, i: (b + 1, 0, 0)) if mod_row0 else (lambda b, i: (0, 0, 0))),
        _resident((1, d)), _resident((d, 2 * d_ff)), _resident((CONV_W, 2 * d_ff)), _resident((1, 2 * d_ff)),
        _resident((d_ff, d)),
    ]
    args = [h, h, h, mod_l, wts["g2"], wts["w_up"], wts["conv_w"], wts["conv_b"], wts["w_down"]]
    if g_final is not None:
        in_specs.append(_resident((1, d)))
        args.append(g_final)
    return pl.pallas_call(
        functools.partial(_ffn_kernel, final_norm=g_final is not None),
        out_shape=jax.ShapeDtypeStruct(h.shape, F32),
        grid=(nb, ns // tm),
        in_specs=in_specs,
        out_specs=tok,
        compiler_params=_params(2),
        name="conv_ffn",
    )(*args)


def _layer_weights(l, g_norm1, w_in, lam, g_diff, g_cq, w_uq, g_ckv, w_ukv, sink, w_branch, w_gate, b_gate,
                   w_o, g_norm2, w_up, conv_w, conv_b, w_down):
    mx = lambda a: a.astype(MXU_DTYPE)
    row = lambda a: a.reshape(1, -1).astype(F32)
    wb = mx(w_branch[l])
    row_src = _wbranch_c_src()
    wb_c = jnp.concatenate([wb[2, int(r):int(r) + DC] for r in row_src[::DC]], axis=0)
    wb = jnp.stack([wb[0], wb[1], wb_c])
    return {
        "g1": row(g_norm1[l]), "w_in": _take_cols(mx(w_in[l]), _in_proj_src()),
        "lam": lam[l].astype(F32), "g_diff": row(g_diff[l]),
        "g_cq": row(g_cq[l]), "w_uq": _take_cols(mx(w_uq[l]), _wuq_src()),
        "g_ckv": row(g_ckv[l]), "w_kv": _take_cols(mx(w_ukv[l]), _wkv_src()),
        "sink": sink[l].astype(F32),
        "w_gate": mx(w_gate[l]), "b_gate": row(b_gate[l]), "w_branch": wb, "w_o": mx(w_o[l]),
        "g2": row(g_norm2[l]), "w_up": mx(w_up[l]), "conv_w": conv_w[l].astype(F32), "conv_b": row(conv_b[l]),
        "w_down": mx(w_down[l]),
    }


def _tile(n, want):
    t = min(n, want)
    assert n % t == 0, (n, t)
    return t


def kernel(x, c, ctx, c_ctx, w_ada, b_ada, g_norm1, w_in, lam, g_diff, g_cq, w_uq, g_ckv, w_ukv, sink, w_branch,
           w_gate, b_gate, w_o, g_norm2, w_up, conv_w, conv_b, w_down, g_final):
    n_batch, n_lat, d = x.shape
    n_ctx = ctx.shape[1]
    n_layers = w_ada.shape[0]
    assert n_lat % GRID_W == 0 and n_lat % WINDOW == 0
    tm_lat, tm_ctx = _tile(n_lat, 512), _tile(n_ctx, 256)
    tm_wide = _tile(n_lat, 1024)
    tq_lat, tq_ctx = _tile(n_lat, 256), _tile(n_ctx, 256)
    assert n_lat >= tq_lat + 2 * WINDOW

    n_rows = -(-(1 + n_batch) // SUBLANES) * SUBLANES
    c_all = jnp.concatenate([c_ctx[None, :], c, jnp.zeros((n_rows - 1 - n_batch, d), c.dtype)], axis=0)
    mod = _modulation(c_all.astype(F32), w_ada, b_ada).reshape(n_layers, n_rows, N_MOD, d)
    tables = _rope_tables(n_lat)

    h, hc = x.astype(F32), ctx.astype(F32)
    for l in range(n_layers):
        ctx_out = l < n_layers - 1
        lam_init = 0.8 - 0.6 * math.exp(-0.3 * l)
        wts = _layer_weights(l, g_norm1, w_in, lam, g_diff, g_cq, w_uq, g_ckv, w_ukv, sink, w_branch, w_gate,
                             b_gate, w_o, g_norm2, w_up, conv_w, conv_b, w_down)
        lat = _inproj(h, mod[l], True, wts, tables, tm_wide, True)
        cx = _inproj(hc, mod[l], False, wts, None, tm_ctx, ctx_out)
        ys = _mixers(lat, cx, lat, wts, lam_init, tq_lat)
        h = _merge(h, mod[l], True, wts, ys, tm_wide)
        h = _ffn(h, mod[l], True, wts, tm_lat, g_final.reshape(1, d).astype(F32) if not ctx_out else None)
        if ctx_out:
            ys_c = _mixers(cx, cx, None, wts, lam_init, tq_ctx)
            hc = _merge(hc, mod[l], False, wts, ys_c, tm_ctx)
            hc = _ffn(hc, mod[l], False, wts, tm_ctx)
    return h
```

```python
import functools
import math

import numpy as np
import jax
import jax.numpy as jnp
from jax import lax
from jax.experimental import pallas as pl
from jax.experimental.pallas import tpu as pltpu

F32 = jnp.float32
MXU_DTYPE = jnp.bfloat16

GRID_W = 64
HA, DA = 4, 64
HB, NOPE_B, ROPE_B, V_B = 8, 64, 32, 64
Q_RANK, KV_RANK = 256, 128
HC, KVC, DC = 8, 2, 64
G_C = HC // KVC
WINDOW = 128
BRANCH_W = 512
N_BRANCH = 3
CONV_W = 3
ROPE_BASE = 10000.0
EPS = 1e-6
NEG_INF = -1e30
N_MOD = 6
MLA_SCALE = (NOPE_B + ROPE_B) ** -0.5
IN_WIDTHS = (HA * 2 * DA, HA * 2 * DA, HA * 2 * DA, Q_RANK, KV_RANK, ROPE_B, HC * DC, KVC * DC, KVC * DC)
IN_OFFS = tuple(int(v) for v in np.cumsum((0,) + IN_WIDTHS))

LANES = 128
SUBLANES = 8
HALF = LANES // 2
VMEM_LIMIT = 56 * 1024 * 1024

SEG_QA, SEG_KA, SEG_VA, SEG_QC = 0, 512, 1024, 1536
SEG_KC, SEG_VC, SEG_CQ, SEG_CKV, SEG_KR = 2048, 2176, 2432, 2688, 2816
NP_IN = 2944
LOG2E = 1.4426950408889634
KEY_BLOCK = 512
SCORE_SLOTS = {"a": 2, "b": 2, "c": 3}
SUB_TILES = {"a": 4, "b": 2, "c": 4}

def _slab64_src(base, unit_of_group):
    src = np.zeros(LANES, np.int64)
    for grp in range(4):
        unit, half = grp % 2, grp // 2
        for r in range(32):
            src[grp * 32 + r] = base + unit_of_group[unit] + half * 32 + r
    return src


def _in_proj_src():
    src = -np.ones(NP_IN, np.int64)
    qa0, ka0, va0, cq0, ckv0, kr0, qc0, kc0, vc0 = IN_OFFS[:9]
    for h in range(HA):
        src[SEG_QA + h * LANES:SEG_QA + (h + 1) * LANES] = _slab64_src(qa0 + h * 2 * DA, (0, DA))
        src[SEG_KA + h * LANES:SEG_KA + (h + 1) * LANES] = _slab64_src(ka0 + h * 2 * DA, (0, DA))
    src[SEG_VA:SEG_VA + 512] = va0 + np.arange(512)
    for g in range(G_C):
        src[SEG_QC + g * LANES:SEG_QC + (g + 1) * LANES] = _slab64_src(qc0, (g * DC, (G_C + g) * DC))
    src[SEG_KC:SEG_KC + LANES] = _slab64_src(kc0, (0, DC))
    for kv in range(KVC):
        v_lo = SEG_VC + kv * LANES + (kv % 2) * HALF
        src[v_lo:v_lo + DC] = vc0 + kv * DC + np.arange(DC)
    src[SEG_CQ:SEG_CQ + Q_RANK] = cq0 + np.arange(Q_RANK)
    src[SEG_CKV:SEG_CKV + KV_RANK] = ckv0 + np.arange(KV_RANK)
    src[SEG_KR:SEG_KR + 16] = kr0 + np.arange(16)
    src[SEG_KR + HALF:SEG_KR + HALF + 16] = kr0 + 16 + np.arange(16)
    return src


def _mla_slab_src(nope0, rope0):
    src = -np.ones(LANES, np.int64)
    if rope0 is not None:
        src[0:16] = rope0 + np.arange(16)
        src[HALF:HALF + 16] = rope0 + 16 + np.arange(16)
    src[16:64] = nope0 + np.arange(48)
    src[80:96] = nope0 + 48 + np.arange(16)
    return src


def _wuq_src():
    src = -np.ones(HB * LANES, np.int64)
    for h in range(HB):
        b0 = h * (NOPE_B + ROPE_B)
        src[h * LANES:(h + 1) * LANES] = _mla_slab_src(b0, b0 + NOPE_B)
    return src


def _wkv_src():
    src = -np.ones(2 * HB * LANES, np.int64)
    for h in range(HB):
        b0 = h * (NOPE_B + V_B)
        src[h * LANES:(h + 1) * LANES] = _mla_slab_src(b0, None)
        v_lo = HB * LANES + h * LANES + (h % 2) * HALF
        src[v_lo:v_lo + V_B] = b0 + NOPE_B + np.arange(V_B)
    return src


def _wbranch_c_src():
    src = np.zeros(BRANCH_W, np.int64)
    for g in range(G_C):
        for kv in range(KVC):
            src[g * LANES + kv * DC:g * LANES + (kv + 1) * DC] = (kv * G_C + g) * DC + np.arange(DC)
    return src


def _take_cols(w, src):
    parts, start = [], 0
    for i in range(1, len(src) + 1):
        if i == len(src) or (src[i] != src[i - 1] + 1 if src[i - 1] >= 0 else src[i] >= 0):
            if src[start] < 0:
                parts.append(jnp.zeros(w.shape[:-1] + (i - start,), w.dtype))
            else:
                parts.append(w[..., int(src[start]):int(src[start]) + i - start])
            start = i
    return jnp.concatenate(parts, axis=-1)


def _rope_tables(n_tok):
    def table(rot_dim):
        n_freq = rot_dim // 4
        inv = ROPE_BASE ** (-jnp.arange(n_freq, dtype=F32) / n_freq)
        n_rows = n_tok // GRID_W
        rows = jnp.repeat(jnp.arange(n_rows, dtype=F32), GRID_W)
        cols = jnp.tile(jnp.arange(GRID_W, dtype=F32), n_rows)
        ang = jnp.concatenate([rows[:, None] * inv, cols[:, None] * inv], axis=-1)
        return jnp.cos(ang), jnp.sin(ang)

    cos64, sin64 = table(DA)
    cosb, sinb = table(ROPE_B)
    cos64_t = jnp.concatenate([cos64] * 4, axis=-1)
    sin64_t = jnp.concatenate([-sin64, -sin64, sin64, sin64], axis=-1)
    ones = jnp.ones((n_tok, HALF - 16), F32)
    zeros = jnp.zeros((n_tok, HALF - 16), F32)
    cosb_t = jnp.concatenate([cosb, ones, cosb, ones], axis=-1)
    sinb_t = jnp.concatenate([-sinb, zeros, sinb, zeros], axis=-1)
    return cos64_t, sin64_t, cosb_t, sinb_t


def _mm(a, b):
    return jnp.dot(a.astype(MXU_DTYPE), b.astype(MXU_DTYPE), preferred_element_type=F32)


def _mm_t(a, b):
    return lax.dot_general(a.astype(MXU_DTYPE), b.astype(MXU_DTYPE), (((1,), (1,)), ((), ())),
                           preferred_element_type=F32)


def _rms(x, g):
    return x * lax.rsqrt(jnp.mean(x * x, axis=-1, keepdims=True) + EPS) * g


def _norm_mod(x, g, shift, scale):
    return _rms(x, g) * (1.0 + scale) + shift


def _sigmoid(x):
    return 1.0 / (1.0 + jnp.exp(-x))


def _rope_slab(x, cos_t, sin_t):
    return x * cos_t + pltpu.roll(x, HALF, axis=1) * sin_t


def _ones_lanes(n_slab):
    lane = lax.broadcasted_iota(jnp.int32, (1, n_slab * LANES), 1)
    slab_parity = (lane // LANES) % 2
    upper_half = (lane % LANES) // HALF
    return jnp.where(slab_parity + upper_half == 1, 1.0, 0.0)


def _normalise_half(o, extra=None):
    den = o if extra is None else o + extra
    return o * pltpu.roll(1.0 / den, HALF, axis=1)


def _resident(shape):
    nd = len(shape)
    return pl.BlockSpec(shape, lambda *_: (0,) * nd, pipeline_mode=pl.Buffered(1))


def _params(n_grid):
    return pltpu.CompilerParams(dimension_semantics=("arbitrary",) * n_grid, vmem_limit_bytes=VMEM_LIMIT)


def _mod_kernel(c_ref, w_ref, b_ref, o_ref):
    c = c_ref[...]
    s = c * _sigmoid(c)
    o_ref[...] = _mm(s, w_ref[...]) + b_ref[...]


def _modulation(c_all, w_ada, b_ada):
    n_layers, d, n_out = w_ada.shape
    rows = c_all.shape[0]
    tn = 1536
    return pl.pallas_call(
        _mod_kernel,
        out_shape=jax.ShapeDtypeStruct((n_layers, rows, n_out), F32),
        grid=(n_layers, n_out // tn),
        in_specs=[
            pl.BlockSpec((rows, d), lambda l, j: (0, 0)),
            pl.BlockSpec((None, d, tn), lambda l, j: (l, 0, j)),
            pl.BlockSpec((None, 1, tn), lambda l, j: (l, 0, j)),
        ],
        out_specs=pl.BlockSpec((None, rows, tn), lambda l, j: (l, 0, j)),
        compiler_params=_params(2),
        name="modulation",
    )(c_all, w_ada, b_ada.reshape(n_layers, 1, n_out))


def _inproj_kernel(*refs, rope, need_q):
    it = iter(refs)
    h_ref, mod_ref, g1_ref, w_ref, gcq_ref, wuq_ref, gckv_ref, wkv_ref = (next(it) for _ in range(8))
    if rope:
        cos64_ref, sin64_ref, cosb_ref, sinb_ref = (next(it) for _ in range(4))
    if need_q:
        qa_ref, qb_ref, qc_ref = (next(it) for _ in range(3))
    ka_ref, va_ref, kb_ref, vb_ref, kc_ref, vc_ref = (next(it) for _ in range(6))

    u = _norm_mod(h_ref[...], g1_ref[...], mod_ref[0:1, :], mod_ref[1:2, :]).astype(MXU_DTYPE)

    def proj(lo, width):
        return jnp.dot(u, w_ref[:, lo:lo + width], preferred_element_type=F32)

    def rot64(x):
        return _rope_slab(x, cos64_ref[...], sin64_ref[...]) if rope else x

    def rotb(x):
        return _rope_slab(x, cosb_ref[...], sinb_ref[...]) if rope else x

    def store_rot64(dst_ref, lo, n_slab, scale):
        p = proj(lo, n_slab * LANES)
        for s in range(n_slab):
            y = rot64(p[:, s * LANES:(s + 1) * LANES])
            if scale != 1.0:
                y = y * scale
            dst_ref[:, s * LANES:(s + 1) * LANES] = y.astype(dst_ref.dtype)

    low = proj(SEG_CQ, Q_RANK + KV_RANK + LANES)
    if need_q:
        store_rot64(qa_ref, SEG_QA, HA, DA ** -0.5 * LOG2E)
        store_rot64(qc_ref, SEG_QC, G_C, DC ** -0.5 * LOG2E)
    store_rot64(ka_ref, SEG_KA, HA, 1.0)
    va_ref[...] = proj(SEG_VA, 512).astype(va_ref.dtype)
    kcvc = proj(SEG_KC, (1 + KVC) * LANES)
    kc_ref[...] = rot64(kcvc[:, :LANES]).astype(kc_ref.dtype)
    vc_ref[...] = (kcvc[:, LANES:] + _ones_lanes(KVC)).astype(vc_ref.dtype)

    if need_q:
        cq = _rms(low[:, :Q_RANK], gcq_ref[...])
        qh = _mm(cq, wuq_ref[...])
        for h in range(HB):
            y = rotb(qh[:, h * LANES:(h + 1) * LANES]) * (MLA_SCALE * LOG2E)
            qb_ref[:, h * LANES:(h + 1) * LANES] = y.astype(qb_ref.dtype)
    ckv = _rms(low[:, Q_RANK:Q_RANK + KV_RANK], gckv_ref[...])
    kvh = _mm(ckv, wkv_ref[...])
    kr = rotb(low[:, Q_RANK + KV_RANK:])
    for h in range(HB):
        kb_ref[:, h * LANES:(h + 1) * LANES] = (kvh[:, h * LANES:(h + 1) * LANES] + kr).astype(kb_ref.dtype)
    vb_ref[...] = (kvh[:, HB * LANES:] + _ones_lanes(HB)).astype(vb_ref.dtype)


def _inproj(h, mod_l, mod_row0, wts, tables, tm, need_q):
    nb, ns, d = h.shape
    rope = tables is not None
    tok = lambda w: pl.BlockSpec((None, tm, w), lambda b, i: (b, i, 0))
    in_specs = [
        tok(d),
        pl.BlockSpec((None, N_MOD, d), (lambda b, i: (b + 1, 0, 0)) if mod_row0 else (lambda b, i: (0, 0, 0))),
        _resident((1, d)), _resident((d, NP_IN)),
        _resident((1, Q_RANK)), _resident((Q_RANK, HB * LANES)),
        _resident((1, KV_RANK)), _resident((KV_RANK, 2 * HB * LANES)),
    ]
    args = [h, mod_l, wts["g1"], wts["w_in"], wts["g_cq"], wts["w_uq"], wts["g_ckv"], wts["w_kv"]]
    if rope:
        in_specs += [pl.BlockSpec((tm, LANES), lambda b, i: (i, 0))] * 4
        args += list(tables)
    names, widths = [], []
    if need_q:
        names += ["qa", "qb", "qc"]
        widths += [512, HB * LANES, 512]
    names += ["ka", "va", "kb", "vb", "kc", "vc"]
    widths += [512, 512, HB * LANES, HB * LANES, LANES, KVC * LANES]
    outs = pl.pallas_call(
        functools.partial(_inproj_kernel, rope=rope, need_q=need_q),
        out_shape=[jax.ShapeDtypeStruct((nb, ns, w), MXU_DTYPE) for w in widths],
        grid=(nb, ns // tm),
        in_specs=in_specs,
        out_specs=[tok(w) for w in widths],
        compiler_params=_params(2),
        name="inproj",
    )(*args)
    return dict(zip(names, outs))


def _score_pass(q, kv_blocks, s_ref, m_ref, m_extra=None):
    n_rows = q.shape[0]
    col = 0
    for idx, (load_k, _, ok, n) in enumerate(kv_blocks):
        s = _mm_t(q, load_k())
        if ok is not None:
            s = jnp.where(ok, s, NEG_INF)
        s_ref[:, col:col + n] = s
        block_max = functools.reduce(jnp.maximum, [s[:, c * LANES:(c + 1) * LANES] for c in range(n // LANES)])
        m_ref[...] = block_max if idx == 0 else jnp.maximum(m_ref[...], block_max)
        col += n
        yield
    m = jnp.max(m_ref[...], axis=-1, keepdims=True)
    if m_extra is not None:
        m = jnp.maximum(m, m_extra)
    m_ref[...] = jnp.broadcast_to(m, (n_rows, LANES))
    return m


def _value_pass(kv_blocks, s_ref, m_ref, l_ref=None):
    col, o = 0, None
    for idx, (_, load_v, _, n) in enumerate(kv_blocks):
        ps = [jnp.exp2(s_ref[:, col + c * LANES:col + (c + 1) * LANES] - m_ref[...]) for c in range(n // LANES)]
        if l_ref is not None:
            block_sum = functools.reduce(lambda a, b: a + b, ps)
            l_ref[...] = block_sum if idx == 0 else l_ref[...] + block_sum
        pv = _mm(jnp.concatenate(ps, axis=-1), load_v())
        o = pv if o is None else o + pv
        col += n
        yield
    l = jnp.sum(l_ref[...], axis=-1, keepdims=True) if l_ref is not None else None
    return o, l


def _step_together(gens):
    results, live = [None] * len(gens), list(range(len(gens)))
    while live:
        for i in list(live):
            try:
                next(gens[i])
            except StopIteration as stop:
                results[i] = stop.value
                live.remove(i)
    return results


def _run_units(units):
    n_stage = len(units[0])
    done = {}
    for t in range(len(units) + n_stage - 1):
        owners = [(t - k, k) for k in range(n_stage) if 0 <= t - k < len(units)]
        gens = [units[i][k]() if k == 0 else units[i][k](done.pop((i, k - 1))) for i, k in owners]
        for owner, result in zip(owners, _step_together(gens)):
            done[owner] = result


def _hold(state):
    return state
    yield


def _key_blocks(kc_ref, vc_ref, kl_ref, vl_ref, k_sl, v_sl):
    blocks = [(lambda: kc_ref[:, k_sl], lambda: vc_ref[:, v_sl], None, kc_ref.shape[0])]
    if kl_ref is not None:
        n_lat = kl_ref.shape[0]
        kb = min(KEY_BLOCK, n_lat)
        for j in range(n_lat // kb):
            rows = slice(j * kb, (j + 1) * kb)
            blocks.append((lambda rows=rows: kl_ref[rows, k_sl], lambda rows=rows: vl_ref[rows, v_sl], None, kb))
    return blocks


def _unit_mask(unit):
    lane = lax.broadcasted_iota(jnp.int32, (1, LANES), 1)
    return jnp.where((lane // 32) % 2 == unit, 1.0, 0.0).astype(MXU_DTYPE)


def _units_a(lam_ref, gd_ref, q_ref, kc_ref, vc_ref, kl_ref, vl_ref, o_ref, s_ref, m_ref, lam_init):
    def ones_slab(n):
        return jnp.ones((n, LANES), MXU_DTYPE)
    tq = s_ref.shape[1] // 2
    lp = lam_ref[...]
    lam = (jnp.exp(jnp.sum(lp[0:1] * lp[1:2], axis=-1, keepdims=True))
           - jnp.exp(jnp.sum(lp[2:3] * lp[3:4], axis=-1, keepdims=True)) + lam_init)
    m0, m1 = _unit_mask(0), _unit_mask(1)

    def unit(j, h):
        sl = slice(h * LANES, (h + 1) * LANES)
        rows = slice(j * tq, (j + 1) * tq)
        slot = (j * HA + h) % s_ref.shape[0]
        blocks = [(load_k, lambda load_v=load_v, n=n: jnp.concatenate([load_v(), ones_slab(n)], axis=1), ok, n)
                  for load_k, load_v, ok, n in _key_blocks(kc_ref, vc_ref, kl_ref, vl_ref, sl, sl)]
        s_h, m_h = s_ref.at[slot], m_ref.at[slot]

        def score():
            q = q_ref[rows, sl]
            qq = jnp.concatenate([q * m0, q * m1], axis=0)
            yield from _score_pass(qq, blocks, s_h, m_h)

        def finish(_):
            o_ext, _ = yield from _value_pass(blocks, s_h, m_h)
            o2 = o_ext[:, :LANES]
            r = 1.0 / o_ext[:, LANES:]
            o = o2[:tq] * r[:tq] - o2[tq:] * (lam * r[tq:])
            o = _rms(o, gd_ref[...]) * (1.0 - lam_init)
            o_ref[rows, sl] = o.astype(o_ref.dtype)

        return (score,) + (_hold,) * (s_ref.shape[0] - 2) + (finish,)

    return [unit(j, h) for j in range(q_ref.shape[0] // tq) for h in range(HA)]


def _attn_b_kernel(*refs, has_lat):
    if has_lat:
        q_ref, kc_ref, vc_ref, kl_ref, vl_ref, o_ref, s_ref, m_ref = refs
    else:
        q_ref, kc_ref, vc_ref, o_ref, s_ref, m_ref = refs
        kl_ref = vl_ref = None
    lane = lax.broadcasted_iota(jnp.int32, (1, LANES), 1)
    tq = s_ref.shape[1]
    o_even = {}

    def unit(j, h):
        sl = slice(h * LANES, (h + 1) * LANES)
        rows = slice(j * tq, (j + 1) * tq)
        slot = (j * HB + h) % s_ref.shape[0]
        blocks = _key_blocks(kc_ref, vc_ref, kl_ref, vl_ref, sl, sl)
        s_h, m_h = s_ref.at[slot], m_ref.at[slot]

        def finish(_):
            o, _ = yield from _value_pass(blocks, s_h, m_h)
            o = _normalise_half(o)
            if h % 2 == 0:
                o_even[(j, h // 2)] = o
            else:
                pair = h // 2
                o_ref[rows, pair * LANES:(pair + 1) * LANES] = jnp.where(
                    lane < HALF, o_even.pop((j, pair)), o).astype(o_ref.dtype)

        score = lambda: _score_pass(q_ref[rows, sl], blocks, s_h, m_h)
        return (score,) + (_hold,) * (s_ref.shape[0] - 2) + (finish,)

    _run_units([unit(j, h) for j in range(q_ref.shape[0] // tq) for h in range(HB)])


def _units_c(sink_ref, q_ref, kc_ref, vc_ref, kl_ref, vl_ref, o_ref, s_ref, m_ref, n_lat):
    has_lat = kl_ref is not None
    tq = s_ref.shape[1]
    n_sub = q_ref.shape[0] // tq
    lane = lax.broadcasted_iota(jnp.int32, (1, LANES), 1)
    masks = (_unit_mask(0), _unit_mask(1))
    o_kv0 = {}

    def window(j):
        band = tq + 2 * WINDOW
        q0 = pl.program_id(1) * q_ref.shape[0] + j * tq
        start = pl.multiple_of(jnp.minimum(jnp.maximum(q0 - WINDOW, 0), n_lat - band), WINDOW)
        qpos = q0 + lax.broadcasted_iota(jnp.int32, (tq, 1), 0)
        kpos = start + lax.broadcasted_iota(jnp.int32, (1, band), 1)
        return start, band, jnp.abs(qpos - kpos) <= WINDOW

    windows = [window(j) for j in range(n_sub)] if has_lat else None

    def unit(j, g, kv):
        sl = slice(g * LANES, (g + 1) * LANES)
        v_sl = slice(kv * LANES, (kv + 1) * LANES)
        rows = slice(j * tq, (j + 1) * tq)
        blocks = [(lambda: kc_ref[...], lambda: vc_ref[:, v_sl], None, kc_ref.shape[0])]
        if has_lat:
            start, band, ok = windows[j]
            blocks.append((lambda: kl_ref[pl.ds(start, band), :], lambda: vl_ref[pl.ds(start, band), v_sl], ok, band))
        slot = ((j * G_C + g) * KVC + kv) % s_ref.shape[0]
        s_u, m_u = s_ref.at[slot], m_ref.at[slot]

        def score():
            sink = jnp.full((1, 1), sink_ref[kv * G_C + g] * LOG2E, F32)
            yield from _score_pass(q_ref[rows, sl] * masks[kv], blocks, s_u, m_u, m_extra=sink)
            return sink

        def finish(sink):
            o, _ = yield from _value_pass(blocks, s_u, m_u)
            o = _normalise_half(o, extra=jnp.exp2(sink - m_u[...]))
            if kv == 0:
                o_kv0[(j, g)] = o
            else:
                o_ref[rows, sl] = jnp.where(lane < HALF, o_kv0.pop((j, g)), o).astype(o_ref.dtype)

        return (score,) + (_hold,) * (s_ref.shape[0] - 2) + (finish,)

    return [unit(j, g, kv) for j in range(n_sub) for g in range(G_C) for kv in range(KVC)]


def _attn_a_kernel(*refs, lam_init, has_lat):
    refs = list(refs)
    if not has_lat:
        refs[5:5] = [None, None]
    _run_units(_units_a(*refs, lam_init))


def _attn_c_kernel(*refs, n_lat, has_lat):
    refs = list(refs)
    if not has_lat:
        refs[4:4] = [None, None]
    _run_units(_units_c(*refs, n_lat))


def _attention(kind, q, kv_ctx, kv_lat, tq, extra_args, extra_specs, **static):
    nb, nq, qw = q.shape
    has_lat = kv_lat is not None
    full = lambda a: pl.BlockSpec((None,) + a.shape[1:], lambda b, i: (b, 0, 0))
    tq_block = tq * SUB_TILES[kind] if nq % (tq * SUB_TILES[kind]) == 0 else tq
    in_specs = list(extra_specs) + [pl.BlockSpec((None, tq_block, qw), lambda b, i: (b, i, 0))]
    in_specs += [full(kv_ctx[0]), full(kv_ctx[1])]
    args = list(extra_args) + [q, kv_ctx[0], kv_ctx[1]]
    n_keys = kv_ctx[0].shape[1]
    if has_lat:
        in_specs += [full(kv_lat[0]), full(kv_lat[1])]
        args += [kv_lat[0], kv_lat[1]]
        n_keys += (tq + 2 * WINDOW) if kind == "c" else kv_lat[0].shape[1]
    n_rows = 2 * tq if kind == "a" else tq
    n_slots = SCORE_SLOTS[kind]
    body = {"a": _attn_a_kernel, "b": _attn_b_kernel, "c": _attn_c_kernel}[kind]
    return pl.pallas_call(
        functools.partial(body, has_lat=has_lat, **static),
        out_shape=jax.ShapeDtypeStruct((nb, nq, BRANCH_W), MXU_DTYPE),
        grid=(nb, nq // tq_block),
        in_specs=in_specs,
        out_specs=pl.BlockSpec((None, tq_block, BRANCH_W), lambda b, i: (b, i, 0)),
        scratch_shapes=[pltpu.VMEM((n_slots, n_rows, n_keys), F32), pltpu.VMEM((n_slots, n_rows, LANES), F32)],
        compiler_params=_params(2),
        name="attn_" + kind,
    )(*args)


def _mixers(qs, ctx_kv, lat_kv, wts, lam_init, tq):
    pick = lambda d, k, v: (d[k], d[v]) if d is not None else None
    ya = _attention("a", qs["qa"], pick(ctx_kv, "ka", "va"), pick(lat_kv, "ka", "va"), tq,
                    [wts["lam"], wts["g_diff"]], [_resident((4, DA)), _resident((1, 2 * DA))], lam_init=lam_init)
    n_q = qs["qb"].shape[1]
    tq_b = 2 * tq if n_q % (2 * tq) == 0 else tq
    yb = _attention("b", qs["qb"], pick(ctx_kv, "kb", "vb"), pick(lat_kv, "kb", "vb"), tq_b, [], [])
    n_lat = lat_kv["kc"].shape[1] if lat_kv is not None else 0
    yc = _attention("c", qs["qc"], pick(ctx_kv, "kc", "vc"), pick(lat_kv, "kc", "vc"), tq,
                    [wts["sink"]], [pl.BlockSpec(memory_space=pltpu.SMEM)], n_lat=n_lat)
    return ya, yb, yc


def _merge_kernel(h_ref, mod_ref, g1_ref, ya_ref, yb_ref, yc_ref, wg_ref, bg_ref, wb_ref, wo_ref, o_ref):
    x = h_ref[...]
    d = x.shape[-1]
    u = _norm_mod(x, g1_ref[...], mod_ref[0:1, :], mod_ref[1:2, :]).astype(MXU_DTYPE)
    acc = None
    for n, y_ref in enumerate((ya_ref, yb_ref, yc_ref)):
        gate = _sigmoid(jnp.dot(u, wg_ref[:, n * d:(n + 1) * d], preferred_element_type=F32)
                        + bg_ref[:, n * d:(n + 1) * d])
        term = gate * jnp.dot(y_ref[...], wb_ref[n], preferred_element_type=F32)
        acc = term if acc is None else acc + term
    o_ref[...] = x + mod_ref[2:3, :] * _mm(acc, wo_ref[...])


def _merge(h, mod_l, mod_row0, wts, ys, tm):
    nb, ns, d = h.shape
    tok = lambda w: pl.BlockSpec((None, tm, w), lambda b, i: (b, i, 0))
    return pl.pallas_call(
        _merge_kernel,
        out_shape=jax.ShapeDtypeStruct(h.shape, F32),
        grid=(nb, ns // tm),
        in_specs=[
            tok(d),
            pl.BlockSpec((None, N_MOD, d), (lambda b, i: (b + 1, 0, 0)) if mod_row0 else (lambda b, i: (0, 0, 0))),
            _resident((1, d)), tok(BRANCH_W), tok(BRANCH_W), tok(BRANCH_W),
            _resident((d, N_BRANCH * d)), _resident((1, N_BRANCH * d)),
            _resident((N_BRANCH, BRANCH_W, d)), _resident((d, d)),
        ],
        out_specs=tok(d),
        compiler_params=_params(2),
        name="merge",
    )(h, mod_l, wts["g1"], *ys, wts["w_gate"], wts["b_gate"], wts["w_branch"], wts["w_o"])


FF_CHUNK = 512
FF_AHEAD = 4


def _ffn_kernel(*refs, final_norm):
    if final_norm:
        h_ref, hp_ref, hn_ref, mod_ref, g2_ref, wu_ref, cw_ref, cb_ref, wd_ref, gf_ref, o_ref = refs
    else:
        h_ref, hp_ref, hn_ref, mod_ref, g2_ref, wu_ref, cw_ref, cb_ref, wd_ref, o_ref = refs
    i, n_tiles = pl.program_id(1), pl.num_programs(1)
    x = h_ref[...]
    tm, d = x.shape
    d_ff = wd_ref.shape[0]
    shift, scale, g2 = mod_ref[3:4, :], mod_ref[4:5, :], g2_ref[...]
    u_prev = jnp.where(i > 0, _norm_mod(hp_ref[...], g2, shift, scale), 0.0)
    u_next = jnp.where(i < n_tiles - 1, _norm_mod(hn_ref[...], g2, shift, scale), 0.0)
    u = jnp.concatenate([u_prev, _norm_mod(x, g2, shift, scale), u_next], axis=0).astype(MXU_DTYPE)
    n_ext = tm + 2 * SUBLANES

    def conv(a, lo, width):
        w = cw_ref[:, lo:lo + width]
        prev = pltpu.roll(a, 1, axis=0)[SUBLANES:SUBLANES + tm]
        nxt = pltpu.roll(a, n_ext - 1, axis=0)[SUBLANES:SUBLANES + tm]
        return (w[0:1] * prev + w[1:2] * a[SUBLANES:SUBLANES + tm] + w[2:3] * nxt) + cb_ref[:, lo:lo + width]

    def up(lo):
        width = min(FF_CHUNK, d_ff - lo)
        return (jnp.dot(u, wu_ref[:, lo:lo + width], preferred_element_type=F32),
                jnp.dot(u, wu_ref[:, d_ff + lo:d_ff + lo + width], preferred_element_type=F32))

    chunks = list(range(0, d_ff, FF_CHUNK))
    acc, ahead = None, [up(lo) for lo in chunks[:FF_AHEAD]]
    for c, lo in enumerate(chunks):
        width = min(FF_CHUNK, d_ff - lo)
        a_gate, a_val = ahead.pop(0)
        if c + FF_AHEAD < len(chunks):
            ahead.append(up(chunks[c + FF_AHEAD]))
        gate = conv(a_gate, lo, width)
        val = conv(a_val, d_ff + lo, width)
        z = gate * _sigmoid(gate) * val
        part = _mm(z, wd_ref[lo:lo + width, :])
        acc = part if acc is None else acc + part
    y = x + mod_ref[5:6, :] * acc
    if final_norm:
        y = _rms(y, gf_ref[...])
    o_ref[...] = y


def _ffn(h, mod_l, mod_row0, wts, tm, g_final=None):
    nb, ns, d = h.shape
    d_ff = wts["w_down"].shape[0]
    per = tm // SUBLANES
    last = ns // SUBLANES - 1
    tok = pl.BlockSpec((None, tm, d), lambda b, i: (b, i, 0))
    in_specs = [
        tok,
        pl.BlockSpec((None, SUBLANES, d), lambda b, i: (b, jnp.maximum(i * per - 1, 0), 0)),
        pl.BlockSpec((None, SUBLANES, d), lambda b, i: (b, jnp.minimum((i + 1) * per, last), 0)),
        pl.BlockSpec((None, N_MOD, d), (lambda b, i: (b + 1, 0, 0)) if mod_row0 else (lambda b, i: (0, 0, 0))),
        _resident((1, d)), _resident((d, 2 * d_ff)), _resident((CONV_W, 2 * d_ff)), _resident((1, 2 * d_ff)),
        _resident((d_ff, d)),
    ]
    args = [h, h, h, mod_l, wts["g2"], wts["w_up"], wts["conv_w"], wts["conv_b"], wts["w_down"]]
    if g_final is not None:
        in_specs.append(_resident((1, d)))
        args.append(g_final)
    return pl.pallas_call(
        functools.partial(_ffn_kernel, final_norm=g_final is not None),
        out_shape=jax.ShapeDtypeStruct(h.shape, F32),
        grid=(nb, ns // tm),
        in_specs=in_specs,
        out_specs=tok,
        compiler_params=_params(2),
        name="conv_ffn",
    )(*args)


def _layer_weights(l, g_norm1, w_in, lam, g_diff, g_cq, w_uq, g_ckv, w_ukv, sink, w_branch, w_gate, b_gate,
                   w_o, g_norm2, w_up, conv_w, conv_b, w_down):
    mx = lambda a: a.astype(MXU_DTYPE)
    row = lambda a: a.reshape(1, -1).astype(F32)
    wb = mx(w_branch[l])
    row_src = _wbranch_c_src()
    wb_c = jnp.concatenate([wb[2, int(r):int(r) + DC] for r in row_src[::DC]], axis=0)
    wb = jnp.stack([wb[0], wb[1], wb_c])
    return {
        "g1": row(g_norm1[l]), "w_in": _take_cols(mx(w_in[l]), _in_proj_src()),
        "lam": lam[l].astype(F32), "g_diff": row(g_diff[l]),
        "g_cq": row(g_cq[l]), "w_uq": _take_cols(mx(w_uq[l]), _wuq_src()),
        "g_ckv": row(g_ckv[l]), "w_kv": _take_cols(mx(w_ukv[l]), _wkv_src()),
        "sink": sink[l].astype(F32),
        "w_gate": mx(w_gate[l]), "b_gate": row(b_gate[l]), "w_branch": wb, "w_o": mx(w_o[l]),
        "g2": row(g_norm2[l]), "w_up": mx(w_up[l]), "conv_w": conv_w[l].astype(F32), "conv_b": row(conv_b[l]),
        "w_down": mx(w_down[l]),
    }


def _tile(n, want):
    t = min(n, want)
    assert n % t == 0, (n, t)
    return t


def kernel(x, c, ctx, c_ctx, w_ada, b_ada, g_norm1, w_in, lam, g_diff, g_cq, w_uq, g_ckv, w_ukv, sink, w_branch,
           w_gate, b_gate, w_o, g_norm2, w_up, conv_w, conv_b, w_down, g_final):
    n_batch, n_lat, d = x.shape
    n_ctx = ctx.shape[1]
    n_layers = w_ada.shape[0]
    assert n_lat % GRID_W == 0 and n_lat % WINDOW == 0
    tm_lat, tm_ctx = _tile(n_lat, 512), _tile(n_ctx, 256)
    tm_wide = _tile(n_lat, 1024)
    tq_lat, tq_ctx = _tile(n_lat, 256), _tile(n_ctx, 256)
    assert n_lat >= tq_lat + 2 * WINDOW

    n_rows = -(-(1 + n_batch) // SUBLANES) * SUBLANES
    c_all = jnp.concatenate([c_ctx[None, :], c, jnp.zeros((n_rows - 1 - n_batch, d), c.dtype)], axis=0)
    mod = _modulation(c_all.astype(F32), w_ada, b_ada).reshape(n_layers, n_rows, N_MOD, d)
    tables = _rope_tables(n_lat)

    h, hc = x.astype(F32), ctx.astype(F32)
    for l in range(n_layers):
        ctx_out = l < n_layers - 1
        lam_init = 0.8 - 0.6 * math.exp(-0.3 * l)
        wts = _layer_weights(l, g_norm1, w_in, lam, g_diff, g_cq, w_uq, g_ckv, w_ukv, sink, w_branch, w_gate,
                             b_gate, w_o, g_norm2, w_up, conv_w, conv_b, w_down)
        lat = _inproj(h, mod[l], True, wts, tables, tm_wide, True)
        cx = _inproj(hc, mod[l], False, wts, None, tm_ctx, ctx_out)
        ys = _mixers(lat, cx, lat, wts, lam_init, tq_lat)
        h = _merge(h, mod[l], True, wts, ys, tm_wide)
        h = _ffn(h, mod[l], True, wts, tm_lat, g_final.reshape(1, d).astype(F32) if not ctx_out else None)
        if ctx_out:
            ys_c = _mixers(cx, cx, None, wts, lam_init, tq_ctx)
            hc = _merge(hc, mod[l], False, wts, ys_c, tm_ctx)
            hc = _ffn(hc, mod[l], False, wts, tm_ctx)
    return h
```
